```python
import math
import jax, jax.numpy as jnp
from jax import lax
import numpy as np

D_MODEL = 2048
BATCH = 16
SEQ = 2048
DEPTH = 1

N_META = 16
D_MIX = D_MODEL
ATTN_WIDTH = D_MIX // 2
MLSTM_WIDTH = D_MIX - ATTN_WIDTH
ATTN_HEAD_DIM = 64
ATTN_HEADS = ATTN_WIDTH // (2 * ATTN_HEAD_DIM)
ATTN_V_DIM = 2 * ATTN_HEAD_DIM
ROPE_THETA = 10000.0
Q_BLOCK = 128
PAD_LEN = Q_BLOCK - N_META
MLSTM_HEADS = 4
MLSTM_V_DIM = MLSTM_WIDTH // MLSTM_HEADS
MLSTM_QK_DIM = MLSTM_V_DIM // 2
MLSTM_CHUNK = 64
CONV_WIDTH = 4
FORGET_BIAS_INIT = 3.0
PEER_HEADS = 8
PEER_N_KEYS = 128
PEER_N_EXPERTS = PEER_N_KEYS * PEER_N_KEYS
PEER_SUB_DIM = 128
PEER_KEY_DIM = 2 * PEER_SUB_DIM
PEER_TOPK = 16
PEER_TOK_BLOCK = 128
EPS = 1e-6
NEG = -1e30

IN_SIZES = (
    ATTN_HEADS * 2 * ATTN_HEAD_DIM,
    ATTN_HEADS * 2 * ATTN_HEAD_DIM,
    ATTN_HEADS * ATTN_V_DIM,
    MLSTM_HEADS * MLSTM_QK_DIM,
    MLSTM_HEADS * MLSTM_QK_DIM,
    MLSTM_HEADS * MLSTM_V_DIM,
    MLSTM_HEADS * MLSTM_V_DIM,
    MLSTM_HEADS,
    MLSTM_HEADS,
)
IN_COLS = sum(IN_SIZES)

kernel_name = 'hybrid_diffattn_mlstm_peer'


def _split_points():
    pts, acc = [], 0
    for s in IN_SIZES[:-1]:
        acc += s
        pts.append(acc)
    return pts


def rmsnorm(x, w):
    xf = x.astype(jnp.float32)
    y = xf * lax.rsqrt(jnp.mean(xf * xf, axis=-1, keepdims=True) + EPS)
    return (y * w.astype(jnp.float32)).astype(x.dtype)


def rope_tables(pos, dim):
    inv_freq = ROPE_THETA ** (-jnp.arange(0, dim, 2, dtype=jnp.float32) / dim)
    ang = pos.astype(jnp.float32)[:, None] * inv_freq[None, :]
    ang = jnp.concatenate([ang, ang], axis=-1)
    return jnp.cos(ang), jnp.sin(ang)


def apply_rope(x, cos, sin):
    c = cos.astype(x.dtype)[None, :, None, None, :]
    s = sin.astype(x.dtype)[None, :, None, None, :]
    x1, x2 = jnp.split(x, 2, axis=-1)
    return x * c + jnp.concatenate([-x2, x1], axis=-1) * s


def causal_conv(x, w, b):
    c = x.shape[-1]
    y = lax.conv_general_dilated(x, w[:, None, :], window_strides=(1,),
                                 padding=[(CONV_WIDTH - 1, 0)],
                                 dimension_numbers=('NWC', 'WIO', 'NWC'),
                                 feature_group_count=c)
    return y + b


def diff_attention(q, k, v, lam_qk, subln_w, lambda_init):
    B, P = q.shape[0], q.shape[1]
    nb = P // Q_BLOCK
    pos = jnp.maximum(jnp.arange(P) - PAD_LEN, 0)
    cos, sin = rope_tables(pos, ATTN_HEAD_DIM)
    q = apply_rope(q, cos, sin) * (ATTN_HEAD_DIM ** -0.5)
    k = apply_rope(k, cos, sin)
    lq = lam_qk.astype(jnp.float32)
    lam = jnp.exp(jnp.sum(lq[0] * lq[1])) - jnp.exp(jnp.sum(lq[2] * lq[3])) + lambda_init
    q_blocks = q.reshape(B, nb, Q_BLOCK, ATTN_HEADS, 2, ATTN_HEAD_DIM).transpose(1, 0, 2, 3, 4, 5)
    key_idx = jnp.arange(P)

    def block(args):
        qb, bi = args
        q_idx = bi * Q_BLOCK + jnp.arange(Q_BLOCK)
        s = jnp.einsum('bqhcd,bkhcd->bhcqk', qb, k).astype(jnp.float32)
        mask = (key_idx[None, :] <= q_idx[:, None]) & (key_idx[None, :] >= PAD_LEN)
        s = jnp.where(mask[None, None, None], s, NEG)
        p = jax.nn.softmax(s, axis=-1)
        a = p[:, :, 0] - lam * p[:, :, 1]
        return jnp.einsum('bhqk,bkhe->bqhe', a.astype(v.dtype), v)

    out = lax.map(block, (q_blocks, jnp.arange(nb)))
    out = out.transpose(1, 0, 2, 3, 4).reshape(B, P, ATTN_HEADS, ATTN_V_DIM)
    out = rmsnorm(out, subln_w) * (1.0 - lambda_init)
    return out.reshape(B, P, ATTN_HEADS * ATTN_V_DIM)


def mlstm_chunkwise(q, k, v, log_i, log_f):
    B, P = q.shape[0], q.shape[1]
    H, CL = MLSTM_HEADS, MLSTM_CHUNK
    nc = P // CL

    def to_chunks(t):
        return t.astype(jnp.float32).reshape(B, nc, CL, H, -1).transpose(0, 3, 1, 2, 4)

    qc = to_chunks(q) * (MLSTM_QK_DIM ** -0.5)
    kc, vc = to_chunks(k), to_chunks(v)
    li = log_i.reshape(B, nc, CL, H).transpose(0, 3, 1, 2)
    lf = log_f.reshape(B, nc, CL, H).transpose(0, 3, 1, 2)
    b = jnp.cumsum(lf, axis=-1)
    g = b[..., -1]
    a = g[..., None] - b + li

    def step(carry, xs):
        C, n, m = carry
        k_s, v_s, a_s, g_s = xs
        m_new = jnp.maximum(g_s + m, jnp.max(a_s, axis=-1))
        w = jnp.exp(a_s - m_new[..., None])
        decay = jnp.exp(g_s + m - m_new)
        C_new = decay[..., None, None] * C + jnp.einsum('bhs,bhsk,bhsv->bhkv', w, k_s, v_s)
        n_new = decay[..., None] * n + jnp.einsum('bhs,bhsk->bhk', w, k_s)
        return (C_new, n_new, m_new), (C, n, m)

    init = (jnp.zeros((B, H, MLSTM_QK_DIM, MLSTM_V_DIM), jnp.float32),
            jnp.zeros((B, H, MLSTM_QK_DIM), jnp.float32),
            jnp.full((B, H), NEG, jnp.float32))
    xs = (kc.transpose(2, 0, 1, 3, 4), vc.transpose(2, 0, 1, 3, 4),
          a.transpose(2, 0, 1, 3), g.transpose(2, 0, 1))
    _, (C_prev, n_prev, m_prev) = lax.scan(step, init, xs)
    C_prev = C_prev.transpose(1, 2, 0, 3, 4)
    n_prev = n_prev.transpose(1, 2, 0, 3)
    m_prev = m_prev.transpose(1, 2, 0)

    causal = jnp.tril(jnp.ones((CL, CL), dtype=bool))
    D = jnp.where(causal, b[..., :, None] - b[..., None, :] + li[..., None, :], NEG)
    m_inter = b + m_prev[..., None]
    m_t = jnp.maximum(jnp.max(D, axis=-1), m_inter)
    S = jnp.einsum('bhctd,bhcsd->bhcts', qc, kc) * jnp.exp(D - m_t[..., None])
    inter_w = jnp.exp(m_inter - m_t)
    num = jnp.einsum('bhcts,bhcsv->bhctv', S, vc) + \
        inter_w[..., None] * jnp.einsum('bhctk,bhckv->bhctv', qc, C_prev)
    nq = jnp.sum(S, axis=-1) + inter_w * jnp.einsum('bhctk,bhck->bhct', qc, n_prev)
    h = num / jnp.maximum(jnp.abs(nq), jnp.exp(-m_t))[..., None]
    return h.transpose(0, 2, 3, 1, 4).reshape(B, P, H, MLSTM_V_DIM)


def peer_ffn(h, w_q, sub_keys, u, v):
    B, L, D = h.shape
    T = B * L
    ht = h.reshape(T, D)
    q = (ht @ w_q).reshape(T, PEER_HEADS, 2, PEER_SUB_DIM)
    s = jnp.einsum('thpd,hpnd->thpn', q, sub_keys).astype(jnp.float32)
    s1, i1 = lax.top_k(s[:, :, 0], PEER_TOPK)
    s2, i2 = lax.top_k(s[:, :, 1], PEER_TOPK)
    cand_s = (s1[..., :, None] + s2[..., None, :]).reshape(T, PEER_HEADS, PEER_TOPK * PEER_TOPK)
    cand_i = (i1[..., :, None] * PEER_N_KEYS + i2[..., None, :]).reshape(T, PEER_HEADS, PEER_TOPK * PEER_TOPK)
    best_s, best_pos = lax.top_k(cand_s, PEER_TOPK)
    idx = jnp.take_along_axis(cand_i, best_pos, axis=-1).reshape(T, PEER_HEADS * PEER_TOPK)
    gate = jax.nn.softmax(best_s, axis=-1).reshape(T, PEER_HEADS * PEER_TOPK).astype(h.dtype)
    nblk = -(-T // PEER_TOK_BLOCK)
    pad = nblk * PEER_TOK_BLOCK - T
    ht_b = jnp.pad(ht, ((0, pad), (0, 0))).reshape(nblk, PEER_TOK_BLOCK, D)
    idx_b = jnp.pad(idx, ((0, pad), (0, 0))).reshape(nblk, PEER_TOK_BLOCK, -1)
    g_b = jnp.pad(gate, ((0, pad), (0, 0))).reshape(nblk, PEER_TOK_BLOCK, -1)

    def block(args):
        hb, ib, gb = args
        ub = jnp.take(u, ib, axis=0)
        act = jax.nn.gelu(jnp.einsum('td,tkd->tk', hb, ub), approximate=False)
        return jnp.einsum('tk,tkd->td', gb * act, jnp.take(v, ib, axis=0))

    y = lax.map(block, (ht_b, idx_b, g_b)).reshape(nblk * PEER_TOK_BLOCK, D)[:T]
    return y.reshape(B, L, D)


def hybrid_layer(x, norm_mix_w, w_in, attn_lambda_qk, attn_subln_w, conv_w, conv_b,
                 i_b, f_b, mlstm_norm_w, w_out, norm_ffn_w, peer_w_q, peer_sub_keys,
                 peer_u, peer_v, lambda_init):
    B, L, _ = x.shape
    P = L + PAD_LEN
    h = rmsnorm(x, norm_mix_w)
    z = jnp.einsum('bld,dc->blc', h, w_in)
    z = jnp.pad(z, ((0, 0), (PAD_LEN, 0), (0, 0)))
    aq, ak, av, mq, mk, mv, mo, mi, mf = jnp.split(z, _split_points(), axis=-1)
    valid = (jnp.arange(P) >= PAD_LEN)[None, :, None]

    attn_out = diff_attention(aq.reshape(B, P, ATTN_HEADS, 2, ATTN_HEAD_DIM),
                              ak.reshape(B, P, ATTN_HEADS, 2, ATTN_HEAD_DIM),
                              av.reshape(B, P, ATTN_HEADS, ATTN_V_DIM),
                              attn_lambda_qk, attn_subln_w, lambda_init)

    qk = jax.nn.silu(causal_conv(jnp.concatenate([mq, mk], axis=-1), conv_w, conv_b))
    mq, mk = jnp.split(qk, 2, axis=-1)
    log_i = jnp.where(valid, mi.astype(jnp.float32) + i_b.astype(jnp.float32), NEG)
    log_f = jnp.where(valid, jax.nn.log_sigmoid(mf.astype(jnp.float32) + f_b.astype(jnp.float32)), 0.0)
    hm = mlstm_chunkwise(mq.reshape(B, P, MLSTM_HEADS, MLSTM_QK_DIM),
                         mk.reshape(B, P, MLSTM_HEADS, MLSTM_QK_DIM),
                         mv.reshape(B, P, MLSTM_HEADS, MLSTM_V_DIM), log_i, log_f)
    hm = rmsnorm(hm, mlstm_norm_w.reshape(MLSTM_HEADS, MLSTM_V_DIM)).astype(x.dtype)
    mlstm_out = (hm * jax.nn.sigmoid(mo.reshape(B, P, MLSTM_HEADS, MLSTM_V_DIM))).reshape(B, P, MLSTM_WIDTH)

    mix = jnp.concatenate([attn_out, mlstm_out], axis=-1)[:, PAD_LEN:]
    x = x + jnp.einsum('blc,cd->bld', mix, w_out)
    x = x + peer_ffn(rmsnorm(x, norm_ffn_w), peer_w_q, peer_sub_keys, peer_u, peer_v)
    return x


def setup_inputs(seed: int = 0) -> dict:
    key = jax.random.key(seed)
    ks = jax.random.split(key, 18)
    f32 = jnp.float32
    nrm = lambda k, shape, scale: scale * jax.random.normal(k, shape, f32)
    return {
        'x': nrm(ks[0], (BATCH, SEQ, D_MODEL), 1.0),
        'meta_tokens': nrm(ks[1], (N_META, D_MODEL), 1.0),
        'norm_mix_w': 1.0 + nrm(ks[2], (DEPTH, D_MODEL), 0.02),
        'w_in': nrm(ks[3], (DEPTH, D_MODEL, IN_COLS), D_MODEL ** -0.5),
        'attn_lambda_qk': nrm(ks[4], (DEPTH, 4, ATTN_HEAD_DIM), 0.1),
        'attn_subln_w': 1.0 + nrm(ks[5], (DEPTH, ATTN_V_DIM), 0.02),
        'mlstm_conv_w': nrm(ks[6], (DEPTH, CONV_WIDTH, 2 * MLSTM_HEADS * MLSTM_QK_DIM), CONV_WIDTH ** -0.5),
        'mlstm_conv_b': nrm(ks[7], (DEPTH, 2 * MLSTM_HEADS * MLSTM_QK_DIM), 0.02),
        'mlstm_i_b': nrm(ks[8], (DEPTH, MLSTM_HEADS), 0.1),
        'mlstm_f_b': FORGET_BIAS_INIT + nrm(ks[9], (DEPTH, MLSTM_HEADS), 0.5),
        'mlstm_norm_w': 1.0 + nrm(ks[10], (DEPTH, MLSTM_WIDTH), 0.02),
        'w_out': nrm(ks[11], (DEPTH, D_MIX, D_MODEL), D_MIX ** -0.5),
        'norm_ffn_w': 1.0 + nrm(ks[12], (DEPTH, D_MODEL), 0.02),
        'peer_w_q': nrm(ks[13], (DEPTH, D_MODEL, PEER_HEADS * PEER_KEY_DIM), D_MODEL ** -0.5),
        'peer_sub_keys': nrm(ks[14], (DEPTH, PEER_HEADS, 2, PEER_N_KEYS, PEER_SUB_DIM), PEER_SUB_DIM ** -0.5),
        'peer_u': nrm(ks[15], (DEPTH, PEER_N_EXPERTS, D_MODEL), D_MODEL ** -0.5),
        'peer_v': nrm(ks[16], (DEPTH, PEER_N_EXPERTS, D_MODEL), D_MODEL ** -0.5),
        'norm_final_w': 1.0 + nrm(ks[17], (D_MODEL,), 0.02),
    }


def reference(x, meta_tokens, norm_mix_w, w_in, attn_lambda_qk, attn_subln_w, mlstm_conv_w,
              mlstm_conv_b, mlstm_i_b, mlstm_f_b, mlstm_norm_w, w_out, norm_ffn_w, peer_w_q,
              peer_sub_keys, peer_u, peer_v, norm_final_w):
    B = x.shape[0]
    meta = jnp.broadcast_to(meta_tokens[None].astype(x.dtype), (B, N_META, D_MODEL))
    h = jnp.concatenate([meta, x], axis=1)
    for layer in range(DEPTH):
        lambda_init = 0.8 - 0.6 * math.exp(-0.3 * layer)
        h = hybrid_layer(h, norm_mix_w[layer], w_in[layer], attn_lambda_qk[layer],
                         attn_subln_w[layer], mlstm_conv_w[layer], mlstm_conv_b[layer],
                         mlstm_i_b[layer], mlstm_f_b[layer], mlstm_norm_w[layer], w_out[layer],
                         norm_ffn_w[layer], peer_w_q[layer], peer_sub_keys[layer],
                         peer_u[layer], peer_v[layer], lambda_init)
    h = rmsnorm(h, norm_final_w)
    return h[:, N_META:]
```

```python
import functools
import math

import jax
import jax.numpy as jnp
from jax import lax
from jax.experimental import pallas as pl
from jax.experimental.pallas import tpu as pltpu

F32 = jnp.float32
BF16 = jnp.bfloat16

N_META = 16
Q_BLOCK = 128
PAD_LEN = Q_BLOCK - N_META
ATTN_HEADS = 8
ATTN_HEAD_DIM = 64
ATTN_V_DIM = 128
ROPE_THETA = 10000.0
MLSTM_HEADS = 4
MLSTM_QK_DIM = 128
MLSTM_V_DIM = 256
MLSTM_CHUNK = 64
CONV_WIDTH = 4
PEER_HEADS = 8
PEER_N_KEYS = 128
PEER_SUB_DIM = 128
PEER_TOPK = 16
EPS = 1e-6
NEG = -1e30

LANES = 128
VMEM_LIMIT = 56 * 1024 * 1024

_NT = (((1,), (1,)), ((), ()))
_TN = (((0,), (0,)), ((), ()))


def _params(sem):
    return pltpu.CompilerParams(dimension_semantics=sem, vmem_limit_bytes=VMEM_LIMIT)


def _inproj_kernel(x_ref, nw_ref, w_ref, wg_ref, z_ref, zg_ref, h_scr):
    @pl.when(pl.program_id(1) == 0)
    def _():
        x = x_ref[...]
        ms = jnp.mean(x * x, axis=-1, keepdims=True)
        h_scr[...] = (x * lax.rsqrt(ms + EPS) * nw_ref[...]).astype(BF16)
        zg_ref[...] = jnp.dot(h_scr[...], wg_ref[...], preferred_element_type=F32)

    z_ref[...] = jnp.dot(h_scr[...], w_ref[...], preferred_element_type=F32).astype(BF16)


def _inproj(hp, norm_w, w_main, w_gate, tm, tn):
    rows, d = hp.shape
    cols = w_main.shape[1]
    return pl.pallas_call(
        _inproj_kernel,
        grid=(rows // tm, cols // tn),
        in_specs=[
            pl.BlockSpec((tm, d), lambda i, j: (i, 0)),
            pl.BlockSpec((1, d), lambda i, j: (0, 0)),
            pl.BlockSpec((d, tn), lambda i, j: (0, j)),
            pl.BlockSpec((d, LANES), lambda i, j: (0, 0)),
        ],
        out_specs=[
            pl.BlockSpec((tm, tn), lambda i, j: (i, j)),
            pl.BlockSpec((tm, LANES), lambda i, j: (i, 0)),
        ],
        out_shape=[
            jax.ShapeDtypeStruct((rows, cols), BF16),
            jax.ShapeDtypeStruct((rows, LANES), F32),
        ],
        scratch_shapes=[pltpu.VMEM((tm, d), BF16)],
        compiler_params=_params(("parallel", "arbitrary")),
        name="inproj",
    )(hp, norm_w, w_main, w_gate)


def _attn_kernel(lq_ref, q_ref, k_ref, v_ref, cos_ref, sin_ref, sw_ref, o_ref,
                 q1_scr, q2_scr, k_scr, *, seq, tq, lambda_init):
    p_len = seq + Q_BLOCK
    lane = lax.broadcasted_iota(jnp.int32, (p_len, LANES), 1)
    low_half = (lane % ATTN_HEAD_DIM) < (ATTN_HEAD_DIM // 2)
    first_map = lane < ATTN_HEAD_DIM

    def rope(x):
        rot = jnp.where(low_half, pltpu.roll(x, LANES - 32, 1), pltpu.roll(x, 32, 1))
        return x * cos_ref[...] + rot * sin_ref[...]

    q = rope(q_ref[0].astype(F32)) * (ATTN_HEAD_DIM ** -0.5)
    q1_scr[...] = jnp.where(first_map, q, 0.0).astype(BF16)
    q2_scr[...] = jnp.where(first_map, 0.0, q).astype(BF16)
    k_scr[...] = rope(k_ref[0].astype(F32)).astype(BF16)

    lq = lq_ref[...]
    lam = (jnp.exp(jnp.sum(lq[0:1] * lq[1:2], axis=-1, keepdims=True))
           - jnp.exp(jnp.sum(lq[2:3] * lq[3:4], axis=-1, keepdims=True)) + lambda_init)

    def step(q1, q2, kt, vt, carry, mask):
        m1, l1, a1, m2, l2, a2 = carry
        s1 = lax.dot_general(q1, kt, _NT, preferred_element_type=F32)
        s2 = lax.dot_general(q2, kt, _NT, preferred_element_type=F32)
        if mask is not None:
            s1 = jnp.where(mask, s1, NEG)
            s2 = jnp.where(mask, s2, NEG)
        n1 = jnp.maximum(m1, jnp.max(s1, axis=-1, keepdims=True))
        n2 = jnp.maximum(m2, jnp.max(s2, axis=-1, keepdims=True))
        p1 = jnp.exp(s1 - n1)
        p2 = jnp.exp(s2 - n2)
        c1 = jnp.exp(m1 - n1)
        c2 = jnp.exp(m2 - n2)
        l1 = c1 * l1 + jnp.sum(p1, axis=-1, keepdims=True)
        l2 = c2 * l2 + jnp.sum(p2, axis=-1, keepdims=True)
        a1 = c1 * a1 + jnp.dot(p1.astype(BF16), vt, preferred_element_type=F32)
        a2 = c2 * a2 + jnp.dot(p2.astype(BF16), vt, preferred_element_type=F32)
        return n1, l1, a1, n2, l2, a2

    meta_mask = lax.broadcasted_iota(jnp.int32, (tq, Q_BLOCK), 1) >= PAD_LEN
    diag_mask = (lax.broadcasted_iota(jnp.int32, (tq, tq), 1)
                 <= lax.broadcasted_iota(jnp.int32, (tq, tq), 0))

    def q_tile(i, _):
        r0 = pl.multiple_of(Q_BLOCK + i * tq, Q_BLOCK)
        q1 = q1_scr[pl.ds(r0, tq), :]
        q2 = q2_scr[pl.ds(r0, tq), :]
        neg = jnp.full((tq, 1), NEG, F32)
        zero = jnp.zeros((tq, 1), F32)
        zacc = jnp.zeros((tq, ATTN_V_DIM), F32)
        carry = (neg, zero, zacc, neg, zero, zacc)
        carry = step(q1, q2, k_scr[0:Q_BLOCK, :], v_ref[0, 0:Q_BLOCK, :], carry, meta_mask)

        def full(j, c):
            k0 = pl.multiple_of(Q_BLOCK + j * tq, Q_BLOCK)
            return step(q1, q2, k_scr[pl.ds(k0, tq), :], v_ref[0, pl.ds(k0, tq), :], c, None)

        carry = lax.fori_loop(0, i, full, carry)
        carry = step(q1, q2, k_scr[pl.ds(r0, tq), :], v_ref[0, pl.ds(r0, tq), :], carry, diag_mask)
        _, l1, a1, _, l2, a2 = carry
        o = a1 / l1 - lam * (a2 / l2)
        ms = jnp.mean(o * o, axis=-1, keepdims=True)
        o = o * lax.rsqrt(ms + EPS) * sw_ref[...] * (1.0 - lambda_init)
        o_ref[0, pl.ds(pl.multiple_of(i * tq, tq), tq), :] = o.astype(BF16)
        return 0

    lax.fori_loop(0, seq // tq, q_tile, 0)


def _attention(zmain, lam_qk, cos, sin, subln_w, batch, seq, lambda_init, tq):
    p_len = seq + Q_BLOCK
    z3 = zmain.reshape(batch, p_len, zmain.shape[-1])
    kern = functools.partial(_attn_kernel, seq=seq, tq=tq, lambda_init=lambda_init)
    blk = lambda off: pl.BlockSpec((1, p_len, LANES), lambda b, h, off=off: (b, 0, off + h))
    const = lambda shape: pl.BlockSpec(shape, lambda b, h: (0,) * len(shape))
    return pl.pallas_call(
        kern,
        grid=(batch, ATTN_HEADS),
        in_specs=[
            const((4, ATTN_HEAD_DIM)),
            blk(0), blk(ATTN_HEADS), blk(2 * ATTN_HEADS),
            const((p_len, LANES)), const((p_len, LANES)), const((1, ATTN_V_DIM)),
        ],
        out_specs=pl.BlockSpec((1, seq, ATTN_V_DIM), lambda b, h: (b, 0, h)),
        out_shape=jax.ShapeDtypeStruct((batch, seq, ATTN_HEADS * ATTN_V_DIM), BF16),
        scratch_shapes=[pltpu.VMEM((p_len, LANES), BF16)] * 3,
        compiler_params=_params(("parallel", "parallel")),
        name="diff_attn",
    )(lam_qk, z3, z3, z3, cos, sin, subln_w)


def _mlstm_kernel(ib_ref, fb_ref, q_ref, k_ref, v_ref, og_ref, cwq_ref, cwk_ref, cbq_ref, cbk_ref,
                  gi_ref, gf_ref, nw_ref, o_ref,
                  pad_scr, q_scr, k_scr, b_scr, li_scr, c_scr, *, seq):
    p_len = seq + Q_BLOCK
    nc = p_len // MLSTM_CHUNK
    cl = MLSTM_CHUNK
    head = pl.program_id(1)

    def conv_silu(x_ref, w_ref, b_ref):
        pad_scr[0:8, :] = jnp.zeros((8, LANES), F32)
        pad_scr[8:8 + p_len, :] = x_ref[0].astype(F32)
        y = b_ref[...] + w_ref[3:4, :] * pad_scr[8:8 + p_len, :]
        for j in range(CONV_WIDTH - 1):
            y = y + w_ref[j:j + 1, :] * pad_scr[5 + j:5 + j + p_len, :]
        return y * (1.0 / (1.0 + jnp.exp(-y)))

    q_scr[...] = (conv_silu(q_ref, cwq_ref, cbq_ref) * (MLSTM_QK_DIM ** -0.5)).astype(BF16)
    k_scr[...] = conv_silu(k_ref, cwk_ref, cbk_ref)

    rows = gi_ref.shape[1]
    pos = (lax.broadcasted_iota(jnp.int32, (rows, cl), 0) * cl
           + lax.broadcasted_iota(jnp.int32, (rows, cl), 1))
    valid = pos >= PAD_LEN
    fpre = gf_ref[0] + fb_ref[head]
    log_f = jnp.minimum(fpre, 0.0) - jnp.log(1.0 + jnp.exp(-jnp.abs(fpre)))
    log_f = jnp.where(valid, log_f, 0.0)
    li_scr[...] = jnp.where(valid, gi_ref[0] + ib_ref[head], NEG)
    upper = (lax.broadcasted_iota(jnp.int32, (cl, cl), 0)
             <= lax.broadcasted_iota(jnp.int32, (cl, cl), 1)).astype(F32)
    b_scr[...] = jnp.dot(log_f, upper, preferred_element_type=F32,
                         precision=lax.Precision.HIGHEST)

    c_scr[...] = jnp.zeros_like(c_scr)
    r_io = lax.broadcasted_iota(jnp.int32, (cl, cl), 0)
    c_io = lax.broadcasted_iota(jnp.int32, (cl, cl), 1)
    eye = r_io == c_io
    tril = c_io <= r_io

    def to_col(row):
        return jnp.sum(jnp.where(eye, row, 0.0), axis=1, keepdims=True)

    def chunk(c, carry):
        m, n = carry
        r0 = pl.multiple_of(c * cl, cl)
        b_r = b_scr[pl.ds(c, 1), :]
        li_r = li_scr[pl.ds(c, 1), :]
        g = b_r[:, cl - 1:cl]
        a_r = g - b_r + li_r
        b_c = to_col(b_r)
        a_c = to_col(a_r)
        dmat = jnp.where(tril, b_c - b_r + li_r, NEG)
        m_inter = b_c + m
        m_t = jnp.maximum(jnp.max(dmat, axis=1, keepdims=True), m_inter)
        qc = q_scr[pl.ds(r0, cl), :]
        kc = k_scr[pl.ds(r0, cl), :]
        vc = v_ref[0, pl.ds(r0, cl), :]
        s = lax.dot_general(qc, kc.astype(BF16), _NT, preferred_element_type=F32) * jnp.exp(dmat - m_t)
        inter_w = jnp.exp(m_inter - m_t)
        num = (jnp.dot(s.astype(BF16), vc, preferred_element_type=F32)
               + inter_w * jnp.dot(qc, c_scr[...].astype(BF16), preferred_element_type=F32))
        nq = (jnp.sum(s, axis=1, keepdims=True)
              + inter_w * jnp.sum(qc.astype(F32) * n, axis=1, keepdims=True))
        h = num / jnp.maximum(jnp.abs(nq), jnp.exp(-m_t))

        @pl.when(c >= Q_BLOCK // cl)
        def _():
            ms = jnp.mean(h * h, axis=-1, keepdims=True)
            hn = h * lax.rsqrt(ms + EPS) * nw_ref[...]
            og = og_ref[0, pl.ds(r0, cl), :].astype(F32)
            out = hn * (1.0 / (1.0 + jnp.exp(-og)))
            o_ref[0, pl.ds(pl.multiple_of(r0 - Q_BLOCK, cl), cl), :] = out.astype(BF16)

        m_new = jnp.maximum(g + m, jnp.max(a_r, axis=1, keepdims=True))
        w_c = jnp.exp(a_c - m_new)
        decay = jnp.exp(g + m - m_new)
        kw = kc * w_c
        c_scr[...] = decay * c_scr[...] + lax.dot_general(
            kw.astype(BF16), vc, _TN, preferred_element_type=F32)
        n_new = decay * n + jnp.sum(kw, axis=0, keepdims=True)
        return m_new, n_new

    lax.fori_loop(0, nc, chunk, (jnp.full((1, 1), NEG, F32), jnp.zeros((1, MLSTM_QK_DIM), F32)))


def _mlstm(zmain, gi_rows, gf_rows, i_b, f_b, conv_w, conv_b, norm_w, batch, seq):
    p_len = seq + Q_BLOCK
    z3 = zmain.reshape(batch, p_len, zmain.shape[-1])
    nh = MLSTM_HEADS
    rows = gi_rows.shape[1]
    q_off = 3 * ATTN_HEADS
    k_off = q_off + nh
    v_off = (k_off + nh) // 2
    o_off = v_off + nh
    smem = pl.BlockSpec(memory_space=pltpu.SMEM)
    return pl.pallas_call(
        functools.partial(_mlstm_kernel, seq=seq),
        grid=(batch, nh),
        in_specs=[
            smem, smem,
            pl.BlockSpec((1, p_len, LANES), lambda b, h: (b, 0, q_off + h)),
            pl.BlockSpec((1, p_len, LANES), lambda b, h: (b, 0, k_off + h)),
            pl.BlockSpec((1, p_len, MLSTM_V_DIM), lambda b, h: (b, 0, v_off + h)),
            pl.BlockSpec((1, p_len, MLSTM_V_DIM), lambda b, h: (b, 0, o_off + h)),
            pl.BlockSpec((CONV_WIDTH, LANES), lambda b, h: (0, h)),
            pl.BlockSpec((CONV_WIDTH, LANES), lambda b, h: (0, nh + h)),
            pl.BlockSpec((1, LANES), lambda b, h: (0, h)),
            pl.BlockSpec((1, LANES), lambda b, h: (0, nh + h)),
            pl.BlockSpec((1, rows, MLSTM_CHUNK), lambda b, h: (b * nh + h, 0, 0)),
            pl.BlockSpec((1, rows, MLSTM_CHUNK), lambda b, h: (b * nh + h, 0, 0)),
            pl.BlockSpec((1, MLSTM_V_DIM), lambda b, h: (0, h)),
        ],
        out_specs=pl.BlockSpec((1, seq, MLSTM_V_DIM), lambda b, h: (b, 0, h)),
        out_shape=jax.ShapeDtypeStruct((batch, seq, nh * MLSTM_V_DIM), BF16),
        scratch_shapes=[
            pltpu.VMEM((p_len + 8, LANES), F32),
            pltpu.VMEM((p_len, LANES), BF16),
            pltpu.VMEM((p_len, LANES), F32),
            pltpu.VMEM((rows, MLSTM_CHUNK), F32),
            pltpu.VMEM((rows, MLSTM_CHUNK), F32),
            pltpu.VMEM((MLSTM_QK_DIM, MLSTM_V_DIM), F32),
        ],
        compiler_params=_params(("parallel", "parallel")),
        name="mlstm",
    )(i_b, f_b, z3, z3, z3, z3, conv_w, conv_w, conv_b, conv_b, gi_rows, gf_rows, norm_w)


def _outproj_kernel(a_ref, m_ref, x_ref, wa_ref, wm_ref, nw_ref, wq_ref, x1_ref, hn_ref, q_ref):
    x1 = (x_ref[...]
          + jnp.dot(a_ref[...], wa_ref[...], preferred_element_type=F32)
          + jnp.dot(m_ref[...], wm_ref[...], preferred_element_type=F32))
    x1_ref[...] = x1
    ms = jnp.mean(x1 * x1, axis=-1, keepdims=True)
    hn = (x1 * lax.rsqrt(ms + EPS) * nw_ref[...]).astype(BF16)
    hn_ref[...] = hn
    q = jnp.dot(hn, wq_ref[...], preferred_element_type=F32).astype(BF16)
    for hp in range(2 * PEER_HEADS):
        q_ref[hp] = q[:, hp * PEER_SUB_DIM:(hp + 1) * PEER_SUB_DIM]


def _outproj(attn, ml, x2, w_attn, w_ml, norm_w, w_q, tm):
    rows, d = x2.shape
    half = attn.shape[1]
    single = pl.Buffered(1)
    return pl.pallas_call(
        _outproj_kernel,
        grid=(rows // tm,),
        in_specs=[
            pl.BlockSpec((tm, half), lambda i: (i, 0)),
            pl.BlockSpec((tm, half), lambda i: (i, 0)),
            pl.BlockSpec((tm, d), lambda i: (i, 0)),
            pl.BlockSpec((half, d), lambda i: (0, 0), pipeline_mode=single),
            pl.BlockSpec((half, d), lambda i: (0, 0), pipeline_mode=single),
            pl.BlockSpec((1, d), lambda i: (0, 0)),
            pl.BlockSpec((d, d), lambda i: (0, 0), pipeline_mode=single),
        ],
        out_specs=[
            pl.BlockSpec((tm, d), lambda i: (i, 0)),
            pl.BlockSpec((tm, d), lambda i: (i, 0)),
            pl.BlockSpec((2 * PEER_HEADS, tm, PEER_SUB_DIM), lambda i: (0, i, 0)),
        ],
        out_shape=[
            jax.ShapeDtypeStruct((rows, d), F32),
            jax.ShapeDtypeStruct((rows, d), BF16),
            jax.ShapeDtypeStruct((2 * PEER_HEADS, rows, PEER_SUB_DIM), BF16),
        ],
        compiler_params=_params(("parallel",)),
        name="outproj",
    )(attn, ml, x2, w_attn, w_ml, norm_w, w_q)


ROUTE_TOK = LANES
W_ROW_STRIDE = PEER_N_KEYS


def _topk_rows(s, payload):
    n_rows = s.shape[0]
    r_io = lax.broadcasted_iota(jnp.int32, s.shape, 0)
    k_io = lax.broadcasted_iota(jnp.int32, (PEER_TOPK, s.shape[1]), 0)
    vals = jnp.zeros((PEER_TOPK, s.shape[1]), F32)
    poss = jnp.zeros((PEER_TOPK, s.shape[1]), jnp.int32)
    pays = jnp.zeros((PEER_TOPK, s.shape[1]), jnp.int32)
    for it in range(PEER_TOPK):
        m = jnp.max(s, axis=0, keepdims=True)
        pos = jnp.min(jnp.where(s == m, r_io, n_rows), axis=0, keepdims=True)
        hit = r_io == pos
        vals = jnp.where(k_io == it, m, vals)
        poss = jnp.where(k_io == it, pos, poss)
        if payload is not None:
            pay = jnp.max(jnp.where(hit, payload, -1), axis=0, keepdims=True)
            pays = jnp.where(k_io == it, pay, pays)
        s = jnp.where(hit, -jnp.inf, s)
    return vals, poss, pays


def _route_kernel(q_ref, keys_ref, w_ref, g_scr, i1_scr, i2_scr, gt_scr, i1t_scr, i2t_scr, wt_scr):
    nk = PEER_N_KEYS

    def head(h, _):
        s1 = lax.dot_general(keys_ref[2 * h], q_ref[2 * h], _NT, preferred_element_type=F32)
        s2 = lax.dot_general(keys_ref[2 * h + 1], q_ref[2 * h + 1], _NT, preferred_element_type=F32)
        v1, p1, _ = _topk_rows(s1, None)
        v2, p2, _ = _topk_rows(s2, None)
        cand_s = jnp.concatenate([v1[i:i + 1] + v2 for i in range(PEER_TOPK)], axis=0)
        cand_i = jnp.concatenate([p1[i:i + 1] * nk + p2 for i in range(PEER_TOPK)], axis=0)
        best, _, idx = _topk_rows(cand_s, cand_i)
        e = jnp.exp(best - best[0:1])
        gate = e / jnp.sum(e, axis=0, keepdims=True)
        r0 = pl.multiple_of(h * PEER_TOPK, PEER_TOPK)
        g_scr[pl.ds(r0, PEER_TOPK), :] = gate
        i1_scr[pl.ds(r0, PEER_TOPK), :] = lax.shift_right_logical(idx, 7).astype(F32)
        i2_scr[pl.ds(r0, PEER_TOPK), :] = lax.bitwise_and(idx, nk - 1).astype(F32)
        return 0

    lax.fori_loop(0, PEER_HEADS, head, 0)

    gt_scr[...] = g_scr[...].T
    i1t_scr[...] = i1_scr[...].T
    i2t_scr[...] = i2_scr[...].T

    sub = lax.broadcasted_iota(jnp.int32, (nk, nk), 0).astype(F32)

    def token(t, _):
        g_row = gt_scr[pl.ds(t, 1), :]
        lhs = jnp.where(i1t_scr[pl.ds(t, 1), :] == sub, g_row, 0.0).astype(BF16)
        rhs = jnp.where(i2t_scr[pl.ds(t, 1), :] == sub, 1.0, 0.0).astype(BF16)
        w_t = lax.dot_general(lhs, rhs, _NT, preferred_element_type=F32)
        wt_scr[pl.ds(pl.multiple_of(t * W_ROW_STRIDE, W_ROW_STRIDE), nk), :] = w_t
        return 0

    lax.fori_loop(0, ROUTE_TOK, token, 0)

    def key_row(a, _):
        w_ref[a] = wt_scr[pl.ds(a, ROUTE_TOK, stride=W_ROW_STRIDE), :].astype(BF16)
        return 0

    lax.fori_loop(0, nk, key_row, 0)


def _route(q3, keys):
    nhp, rows, sd = q3.shape
    nk = PEER_N_KEYS
    return pl.pallas_call(
        _route_kernel,
        grid=(rows // ROUTE_TOK,),
        in_specs=[
            pl.BlockSpec((nhp, ROUTE_TOK, sd), lambda i: (0, i, 0)),
            pl.BlockSpec((nhp, nk, sd), lambda i: (0, 0, 0)),
        ],
        out_specs=pl.BlockSpec((nk, ROUTE_TOK, nk), lambda i: (0, i, 0)),
        out_shape=jax.ShapeDtypeStruct((nk, rows, nk), BF16),
        scratch_shapes=[pltpu.VMEM((PEER_HEADS * PEER_TOPK, ROUTE_TOK), F32)] * 3
        + [pltpu.VMEM((ROUTE_TOK, PEER_HEADS * PEER_TOPK), F32)] * 3
        + [pltpu.VMEM((ROUTE_TOK * W_ROW_STRIDE, nk), F32)],
        compiler_params=_params(("parallel",)),
        name="peer_route",
    )(q3, keys)


def _expert_kernel(hn_ref, u_ref, v_ref, w_ref, x1_ref, nw_ref, o_ref, acc_scr):
    j = pl.program_id(1)

    @pl.when(j == 0)
    def _():
        acc_scr[...] = x1_ref[...]

    act = lax.dot_general(hn_ref[...], u_ref[...], _NT, preferred_element_type=F32)
    gel = 0.5 * act * (1.0 + lax.erf(act * (2.0 ** -0.5)))
    wts = jnp.concatenate([w_ref[a] for a in range(w_ref.shape[0])], axis=1)
    coef = (gel * wts.astype(F32)).astype(BF16)
    acc_scr[...] += jnp.dot(coef, v_ref[...], preferred_element_type=F32)

    @pl.when(j == pl.num_programs(1) - 1)
    def _():
        y = acc_scr[...]
        ms = jnp.mean(y * y, axis=-1, keepdims=True)
        o_ref[...] = y * lax.rsqrt(ms + EPS) * nw_ref[...]


def _experts(hn, u, v, w3, x1, norm_w, tb, te):
    rows, d = hn.shape
    ne = u.shape[0]
    ka = te // PEER_N_KEYS
    return pl.pallas_call(
        _expert_kernel,
        grid=(rows // tb, ne // te),
        in_specs=[
            pl.BlockSpec((tb, d), lambda i, j: (i, 0)),
            pl.BlockSpec((te, d), lambda i, j: (j, 0)),
            pl.BlockSpec((te, d), lambda i, j: (j, 0)),
            pl.BlockSpec((ka, tb, PEER_N_KEYS), lambda i, j: (j, i, 0)),
            pl.BlockSpec((tb, d), lambda i, j: (i, 0)),
            pl.BlockSpec((1, d), lambda i, j: (0, 0)),
        ],
        out_specs=pl.BlockSpec((tb, d), lambda i, j: (i, 0)),
        out_shape=jax.ShapeDtypeStruct((rows, d), F32),
        scratch_shapes=[pltpu.VMEM((tb, d), F32)],
        compiler_params=_params(("parallel", "arbitrary")),
        name="peer_experts",
    )(hn, u, v, w3, x1, norm_w)


def _rope_tables(p_len):
    pos = jnp.maximum(jnp.arange(p_len) - PAD_LEN, 0).astype(F32)
    inv_freq = ROPE_THETA ** (-jnp.arange(0, ATTN_HEAD_DIM, 2, dtype=F32) / ATTN_HEAD_DIM)
    ang = pos[:, None] * inv_freq[None, :]
    cos = jnp.tile(jnp.cos(ang), (1, 4))
    sin = jnp.sin(ang)
    return cos, jnp.tile(jnp.concatenate([-sin, sin], axis=-1), (1, 2))


def _pick(n, candidates):
    for c in candidates:
        if n % c == 0:
            return c
    raise ValueError(f"no tile in {candidates} divides {n}")


def kernel(x, meta_tokens, norm_mix_w, w_in, attn_lambda_qk, attn_subln_w, mlstm_conv_w, mlstm_conv_b, mlstm_i_b, mlstm_f_b, mlstm_norm_w, w_out, norm_ffn_w, peer_w_q, peer_sub_keys, peer_u, peer_v, norm_final_w):
    batch, seq, d = x.shape
    assert w_in.shape[0] == 1, "single-layer block"
    p_len = seq + Q_BLOCK
    lambda_init = 0.8 - 0.6 * math.exp(-0.3 * 0)
    n_main = w_in.shape[-1] - 2 * MLSTM_HEADS

    head_rows = jnp.concatenate([jnp.zeros((PAD_LEN, d), x.dtype), meta_tokens.astype(x.dtype)], axis=0)
    hp = jnp.concatenate([jnp.broadcast_to(head_rows[None], (batch, Q_BLOCK, d)), x], axis=1)
    hp = hp.reshape(batch * p_len, d)

    w_main = w_in[0, :, :n_main].astype(BF16)
    w_gate = jnp.pad(w_in[0, :, n_main:], ((0, 0), (0, LANES - 2 * MLSTM_HEADS))).astype(BF16)
    zmain, zgate = _inproj(hp, norm_mix_w, w_main, w_gate,
                           tm=_pick(batch * p_len, (1024, 512, 256, 128)), tn=_pick(n_main, (1024, 512)))

    cos, sin = _rope_tables(p_len)
    attn = _attention(zmain, attn_lambda_qk[0], cos, sin, attn_subln_w, batch, seq, lambda_init,
                      tq=_pick(seq, (256, 128)))

    nc = p_len // MLSTM_CHUNK
    rows = -(-nc // 8) * 8
    gates = zgate[:, :2 * MLSTM_HEADS].reshape(batch, nc, MLSTM_CHUNK, 2, MLSTM_HEADS)
    gates = jnp.pad(gates.transpose(3, 0, 4, 1, 2), ((0, 0), (0, 0), (0, 0), (0, rows - nc), (0, 0)))
    gates = gates.reshape(2, batch * MLSTM_HEADS, rows, MLSTM_CHUNK)
    ml = _mlstm(zmain, gates[0], gates[1], mlstm_i_b[0], mlstm_f_b[0], mlstm_conv_w[0], mlstm_conv_b,
                mlstm_norm_w, batch, seq)

    half = ATTN_HEADS * ATTN_V_DIM
    x1, hn, q3 = _outproj(attn.reshape(batch * seq, half), ml.reshape(batch * seq, d - half),
                          x.reshape(batch * seq, d), w_out[0, :half].astype(BF16), w_out[0, half:].astype(BF16),
                          norm_ffn_w, peer_w_q[0].astype(BF16), tm=_pick(batch * seq, (512, 256, 128)))

    keys = peer_sub_keys[0].reshape(2 * PEER_HEADS, PEER_N_KEYS, PEER_SUB_DIM).astype(BF16)
    w3 = _route(q3, keys)

    out = _experts(hn, peer_u[0].astype(BF16), peer_v[0].astype(BF16), w3, x1, norm_final_w[None],
                   tb=_pick(batch * seq, (512, 256, 128)), te=512)
    return out.reshape(batch, seq, d)
```

```python
import functools
import math

import jax
import jax.numpy as jnp
from jax import lax
from jax.experimental import pallas as pl
from jax.experimental.pallas import tpu as pltpu

F32 = jnp.float32
BF16 = jnp.bfloat16

N_META = 16
Q_BLOCK = 128
PAD_LEN = Q_BLOCK - N_META
ATTN_HEADS = 8
ATTN_HEAD_DIM = 64
ATTN_V_DIM = 128
ROPE_THETA = 10000.0
MLSTM_HEADS = 4
MLSTM_QK_DIM = 128
MLSTM_V_DIM = 256
MLSTM_CHUNK = 64
CONV_WIDTH = 4
PEER_HEADS = 8
PEER_N_KEYS = 128
PEER_SUB_DIM = 128
PEER_TOPK = 16
EPS = 1e-6
NEG = -1e30

LANES = 128
VMEM_LIMIT = 56 * 1024 * 1024

_NT = (((1,), (1,)), ((), ()))
_TN = (((0,), (0,)), ((), ()))


def _params(sem):
    return pltpu.CompilerParams(dimension_semantics=sem, vmem_limit_bytes=VMEM_LIMIT)


def _inproj_kernel(x_ref, nw_ref, w_ref, wg_ref, z_ref, zg_ref, h_scr):
    @pl.when(pl.program_id(1) == 0)
    def _():
        x = x_ref[...]
        ms = jnp.mean(x * x, axis=-1, keepdims=True)
        h_scr[...] = (x * lax.rsqrt(ms + EPS) * nw_ref[...]).astype(BF16)
        zg_ref[...] = jnp.dot(h_scr[...], wg_ref[...], preferred_element_type=F32)

    z_ref[...] = jnp.dot(h_scr[...], w_ref[...], preferred_element_type=F32).astype(BF16)


def _inproj(hp, norm_w, w_main, w_gate, tm, tn):
    rows, d = hp.shape
    cols = w_main.shape[1]
    return pl.pallas_call(
        _inproj_kernel,
        grid=(rows // tm, cols // tn),
        in_specs=[
            pl.BlockSpec((tm, d), lambda i, j: (i, 0)),
            pl.BlockSpec((1, d), lambda i, j: (0, 0)),
            pl.BlockSpec((d, tn), lambda i, j: (0, j)),
            pl.BlockSpec((d, LANES), lambda i, j: (0, 0)),
        ],
        out_specs=[
            pl.BlockSpec((tm, tn), lambda i, j: (i, j)),
            pl.BlockSpec((tm, LANES), lambda i, j: (i, 0)),
        ],
        out_shape=[
            jax.ShapeDtypeStruct((rows, cols), BF16),
            jax.ShapeDtypeStruct((rows, LANES), F32),
        ],
        scratch_shapes=[pltpu.VMEM((tm, d), BF16)],
        compiler_params=_params(("parallel", "arbitrary")),
        name="inproj",
    )(hp, norm_w, w_main, w_gate)


def _attn_kernel(lq_ref, q_ref, k_ref, v_ref, cos_ref, sin_ref, sw_ref, o_ref,
                 q1_scr, q2_scr, k_scr, *, seq, tq, lambda_init):
    p_len = seq + Q_BLOCK
    lane = lax.broadcasted_iota(jnp.int32, (p_len, LANES), 1)
    low_half = (lane % ATTN_HEAD_DIM) < (ATTN_HEAD_DIM // 2)
    first_map = lane < ATTN_HEAD_DIM

    def rope(x):
        rot = jnp.where(low_half, pltpu.roll(x, LANES - 32, 1), pltpu.roll(x, 32, 1))
        return x * cos_ref[...] + rot * sin_ref[...]

    q = rope(q_ref[0].astype(F32)) * (ATTN_HEAD_DIM ** -0.5)
    q1_scr[...] = jnp.where(first_map, q, 0.0).astype(BF16)
    q2_scr[...] = jnp.where(first_map, 0.0, q).astype(BF16)
    k_scr[...] = rope(k_ref[0].astype(F32)).astype(BF16)

    lq = lq_ref[...]
    lam = (jnp.exp(jnp.sum(lq[0:1] * lq[1:2], axis=-1, keepdims=True))
           - jnp.exp(jnp.sum(lq[2:3] * lq[3:4], axis=-1, keepdims=True)) + lambda_init)

    def step(q1, q2, kt, vt, carry, mask):
        m1, l1, a1, m2, l2, a2 = carry
        s1 = lax.dot_general(q1, kt, _NT, preferred_element_type=F32)
        s2 = lax.dot_general(q2, kt, _NT, preferred_element_type=F32)
        if mask is not None:
            s1 = jnp.where(mask, s1, NEG)
            s2 = jnp.where(mask, s2, NEG)
        n1 = jnp.maximum(m1, jnp.max(s1, axis=-1, keepdims=True))
        n2 = jnp.maximum(m2, jnp.max(s2, axis=-1, keepdims=True))
        p1 = jnp.exp(s1 - n1)
        p2 = jnp.exp(s2 - n2)
        c1 = jnp.exp(m1 - n1)
        c2 = jnp.exp(m2 - n2)
        l1 = c1 * l1 + jnp.sum(p1, axis=-1, keepdims=True)
        l2 = c2 * l2 + jnp.sum(p2, axis=-1, keepdims=True)
        a1 = c1 * a1 + jnp.dot(p1.astype(BF16), vt, preferred_element_type=F32)
        a2 = c2 * a2 + jnp.dot(p2.astype(BF16), vt, preferred_element_type=F32)
        return n1, l1, a1, n2, l2, a2

    meta_mask = lax.broadcasted_iota(jnp.int32, (tq, Q_BLOCK), 1) >= PAD_LEN
    diag_mask = (lax.broadcasted_iota(jnp.int32, (tq, tq), 1)
                 <= lax.broadcasted_iota(jnp.int32, (tq, tq), 0))

    def q_tile(i, _):
        r0 = pl.multiple_of(Q_BLOCK + i * tq, Q_BLOCK)
        q1 = q1_scr[pl.ds(r0, tq), :]
        q2 = q2_scr[pl.ds(r0, tq), :]
        neg = jnp.full((tq, 1), NEG, F32)
        zero = jnp.zeros((tq, 1), F32)
        zacc = jnp.zeros((tq, ATTN_V_DIM), F32)
        carry = (neg, zero, zacc, neg, zero, zacc)
        carry = step(q1, q2, k_scr[0:Q_BLOCK, :], v_ref[0, 0:Q_BLOCK, :], carry, meta_mask)

        def full(j, c):
            k0 = pl.multiple_of(Q_BLOCK + j * tq, Q_BLOCK)
            return step(q1, q2, k_scr[pl.ds(k0, tq), :], v_ref[0, pl.ds(k0, tq), :], c, None)

        carry = lax.fori_loop(0, i, full, carry)
        carry = step(q1, q2, k_scr[pl.ds(r0, tq), :], v_ref[0, pl.ds(r0, tq), :], carry, diag_mask)
        _, l1, a1, _, l2, a2 = carry
        o = a1 / l1 - lam * (a2 / l2)
        ms = jnp.mean(o * o, axis=-1, keepdims=True)
        o = o * lax.rsqrt(ms + EPS) * sw_ref[...] * (1.0 - lambda_init)
        o_ref[0, pl.ds(pl.multiple_of(i * tq, tq), tq), :] = o.astype(BF16)
        return 0

    lax.fori_loop(0, seq // tq, q_tile, 0)


def _attention(zmain, lam_qk, cos, sin, subln_w, batch, seq, lambda_init, tq):
    p_len = seq + Q_BLOCK
    z3 = zmain.reshape(batch, p_len, zmain.shape[-1])
    kern = functools.partial(_attn_kernel, seq=seq, tq=tq, lambda_init=lambda_init)
    blk = lambda off: pl.BlockSpec((1, p_len, LANES), lambda b, h, off=off: (b, 0, off + h))
    const = lambda shape: pl.BlockSpec(shape, lambda b, h: (0,) * len(shape))
    return pl.pallas_call(
        kern,
        grid=(batch, ATTN_HEADS),
        in_specs=[
            const((4, ATTN_HEAD_DIM)),
            blk(0), blk(ATTN_HEADS), blk(2 * ATTN_HEADS),
            const((p_len, LANES)), const((p_len, LANES)), const((1, ATTN_V_DIM)),
        ],
        out_specs=pl.BlockSpec((1, seq, ATTN_V_DIM), lambda b, h: (b, 0, h)),
        out_shape=jax.ShapeDtypeStruct((batch, seq, ATTN_HEADS * ATTN_V_DIM), BF16),
        scratch_shapes=[pltpu.VMEM((p_len, LANES), BF16)] * 3,
        compiler_params=_params(("parallel", "parallel")),
        name="diff_attn",
    )(lam_qk, z3, z3, z3, cos, sin, subln_w)


def _mlstm_kernel(ib_ref, fb_ref, q_ref, k_ref, v_ref, og_ref, cwq_ref, cwk_ref, cbq_ref, cbk_ref,
                  gi_ref, gf_ref, nw_ref, o_ref,
                  pad_scr, q_scr, k_scr, b_scr, li_scr, c_scr, *, seq):
    p_len = seq + Q_BLOCK
    nc = p_len // MLSTM_CHUNK
    cl = MLSTM_CHUNK
    head = pl.program_id(1)

    def conv_silu(x_ref, w_ref, b_ref):
        pad_scr[0:8, :] = jnp.zeros((8, LANES), F32)
        pad_scr[8:8 + p_len, :] = x_ref[0].astype(F32)
        y = b_ref[...] + w_ref[3:4, :] * pad_scr[8:8 + p_len, :]
        for j in range(CONV_WIDTH - 1):
            y = y + w_ref[j:j + 1, :] * pad_scr[5 + j:5 + j + p_len, :]
        return y * (1.0 / (1.0 + jnp.exp(-y)))

    q_scr[...] = (conv_silu(q_ref, cwq_ref, cbq_ref) * (MLSTM_QK_DIM ** -0.5)).astype(BF16)
    k_scr[...] = conv_silu(k_ref, cwk_ref, cbk_ref)

    rows = gi_ref.shape[1]
    pos = (lax.broadcasted_iota(jnp.int32, (rows, cl), 0) * cl
           + lax.broadcasted_iota(jnp.int32, (rows, cl), 1))
    valid = pos >= PAD_LEN
    fpre = gf_ref[0] + fb_ref[head]
    log_f = jnp.minimum(fpre, 0.0) - jnp.log(1.0 + jnp.exp(-jnp.abs(fpre)))
    log_f = jnp.where(valid, log_f, 0.0)
    li_scr[...] = jnp.where(valid, gi_ref[0] + ib_ref[head], NEG)
    upper = (lax.broadcasted_iota(jnp.int32, (cl, cl), 0)
             <= lax.broadcasted_iota(jnp.int32, (cl, cl), 1)).astype(F32)
    b_scr[...] = jnp.dot(log_f, upper, preferred_element_type=F32,
                         precision=lax.Precision.HIGHEST)

    c_scr[...] = jnp.zeros_like(c_scr)
    r_io = lax.broadcasted_iota(jnp.int32, (cl, cl), 0)
    c_io = lax.broadcasted_iota(jnp.int32, (cl, cl), 1)
    eye = r_io == c_io
    tril = c_io <= r_io

    def to_col(row):
        return jnp.sum(jnp.where(eye, row, 0.0), axis=1, keepdims=True)

    def chunk(c, carry):
        m, n = carry
        r0 = pl.multiple_of(c * cl, cl)
        b_r = b_scr[pl.ds(c, 1), :]
        li_r = li_scr[pl.ds(c, 1), :]
        g = b_r[:, cl - 1:cl]
        a_r = g - b_r + li_r
        b_c = to_col(b_r)
        a_c = to_col(a_r)
        dmat = jnp.where(tril, b_c - b_r + li_r, NEG)
        m_inter = b_c + m
        m_t = jnp.maximum(jnp.max(dmat, axis=1, keepdims=True), m_inter)
        qc = q_scr[pl.ds(r0, cl), :]
        kc = k_scr[pl.ds(r0, cl), :]
        vc = v_ref[0, pl.ds(r0, cl), :]
        s = lax.dot_general(qc, kc.astype(BF16), _NT, preferred_element_type=F32) * jnp.exp(dmat - m_t)
        inter_w = jnp.exp(m_inter - m_t)
        num = (jnp.dot(s.astype(BF16), vc, preferred_element_type=F32)
               + inter_w * jnp.dot(qc, c_scr[...].astype(BF16), preferred_element_type=F32))
        nq = (jnp.sum(s, axis=1, keepdims=True)
              + inter_w * jnp.sum(qc.astype(F32) * n, axis=1, keepdims=True))
        h = num / jnp.maximum(jnp.abs(nq), jnp.exp(-m_t))

        @pl.when(c >= Q_BLOCK // cl)
        def _():
            ms = jnp.mean(h * h, axis=-1, keepdims=True)
            hn = h * lax.rsqrt(ms + EPS) * nw_ref[...]
            og = og_ref[0, pl.ds(r0, cl), :].astype(F32)
            out = hn * (1.0 / (1.0 + jnp.exp(-og)))
            o_ref[0, pl.ds(pl.multiple_of(r0 - Q_BLOCK, cl), cl), :] = out.astype(BF16)

        m_new = jnp.maximum(g + m, jnp.max(a_r, axis=1, keepdims=True))
        w_c = jnp.exp(a_c - m_new)
        decay = jnp.exp(g + m - m_new)
        kw = kc * w_c
        c_scr[...] = decay * c_scr[...] + lax.dot_general(
            kw.astype(BF16), vc, _TN, preferred_element_type=F32)
        n_new = decay * n + jnp.sum(kw, axis=0, keepdims=True)
        return m_new, n_new

    lax.fori_loop(0, nc, chunk, (jnp.full((1, 1), NEG, F32), jnp.zeros((1, MLSTM_QK_DIM), F32)))


def _mlstm(zmain, gi_rows, gf_rows, i_b, f_b, conv_w, conv_b, norm_w, batch, seq):
    p_len = seq + Q_BLOCK
    z3 = zmain.reshape(batch, p_len, zmain.shape[-1])
    nh = MLSTM_HEADS
    rows = gi_rows.shape[1]
    q_off = 3 * ATTN_HEADS
    k_off = q_off + nh
    v_off = (k_off + nh) // 2
    o_off = v_off + nh
    smem = pl.BlockSpec(memory_space=pltpu.SMEM)
    return pl.pallas_call(
        functools.partial(_mlstm_kernel, seq=seq),
        grid=(batch, nh),
        in_specs=[
            smem, smem,
            pl.BlockSpec((1, p_len, LANES), lambda b, h: (b, 0, q_off + h)),
            pl.BlockSpec((1, p_len, LANES), lambda b, h: (b, 0, k_off + h)),
            pl.BlockSpec((1, p_len, MLSTM_V_DIM), lambda b, h: (b, 0, v_off + h)),
            pl.BlockSpec((1, p_len, MLSTM_V_DIM), lambda b, h: (b, 0, o_off + h)),
            pl.BlockSpec((CONV_WIDTH, LANES), lambda b, h: (0, h)),
            pl.BlockSpec((CONV_WIDTH, LANES), lambda b, h: (0, nh + h)),
            pl.BlockSpec((1, LANES), lambda b, h: (0, h)),
            pl.BlockSpec((1, LANES), lambda b, h: (0, nh + h)),
            pl.BlockSpec((1, rows, MLSTM_CHUNK), lambda b, h: (b * nh + h, 0, 0)),
            pl.BlockSpec((1, rows, MLSTM_CHUNK), lambda b, h: (b * nh + h, 0, 0)),
            pl.BlockSpec((1, MLSTM_V_DIM), lambda b, h: (0, h)),
        ],
        out_specs=pl.BlockSpec((1, seq, MLSTM_V_DIM), lambda b, h: (b, 0, h)),
        out_shape=jax.ShapeDtypeStruct((batch, seq, nh * MLSTM_V_DIM), BF16),
        scratch_shapes=[
            pltpu.VMEM((p_len + 8, LANES), F32),
            pltpu.VMEM((p_len, LANES), BF16),
            pltpu.VMEM((p_len, LANES), F32),
            pltpu.VMEM((rows, MLSTM_CHUNK), F32),
            pltpu.VMEM((rows, MLSTM_CHUNK), F32),
            pltpu.VMEM((MLSTM_QK_DIM, MLSTM_V_DIM), F32),
        ],
        compiler_params=_params(("parallel", "parallel")),
        name="mlstm",
    )(i_b, f_b, z3, z3, z3, z3, conv_w, conv_w, conv_b, conv_b, gi_rows, gf_rows, norm_w)


def _outproj_kernel(a_ref, m_ref, x_ref, wa_ref, wm_ref, nw_ref, wq_ref, x1_ref, hn_ref, q_ref):
    x1 = (x_ref[...]
          + jnp.dot(a_ref[...], wa_ref[...], preferred_element_type=F32)
          + jnp.dot(m_ref[...], wm_ref[...], preferred_element_type=F32))
    x1_ref[...] = x1
    ms = jnp.mean(x1 * x1, axis=-1, keepdims=True)
    hn = (x1 * lax.rsqrt(ms + EPS) * nw_ref[...]).astype(BF16)
    hn_ref[...] = hn
    q = jnp.dot(hn, wq_ref[...], preferred_element_type=F32).astype(BF16)
    for hp in range(2 * PEER_HEADS):
        q_ref[hp] = q[:, hp * PEER_SUB_DIM:(hp + 1) * PEER_SUB_DIM]


def _outproj(attn, ml, x2, w_attn, w_ml, norm_w, w_q, tm):
    rows, d = x2.shape
    half = attn.shape[1]
    single = pl.Buffered(1)
    return pl.pallas_call(
        _outproj_kernel,
        grid=(rows // tm,),
        in_specs=[
            pl.BlockSpec((tm, half), lambda i: (i, 0)),
            pl.BlockSpec((tm, half), lambda i: (i, 0)),
            pl.BlockSpec((tm, d), lambda i: (i, 0)),
            pl.BlockSpec((half, d), lambda i: (0, 0), pipeline_mode=single),
            pl.BlockSpec((half, d), lambda i: (0, 0), pipeline_mode=single),
            pl.BlockSpec((1, d), lambda i: (0, 0)),
            pl.BlockSpec((d, d), lambda i: (0, 0), pipeline_mode=single),
        ],
        out_specs=[
            pl.BlockSpec((tm, d), lambda i: (i, 0)),
            pl.BlockSpec((tm, d), lambda i: (i, 0)),
            pl.BlockSpec((2 * PEER_HEADS, tm, PEER_SUB_DIM), lambda i: (0, i, 0)),
        ],
        out_shape=[
            jax.ShapeDtypeStruct((rows, d), F32),
            jax.ShapeDtypeStruct((rows, d), BF16),
            jax.ShapeDtypeStruct((2 * PEER_HEADS, rows, PEER_SUB_DIM), BF16),
        ],
        compiler_params=_params(("parallel",)),
        name="outproj",
    )(attn, ml, x2, w_attn, w_ml, norm_w, w_q)


ROUTE_TOK = LANES
W_ROW_STRIDE = PEER_N_KEYS + 8
SUBLANES = 8


def _topk_rows(s, pos_rows, payload):
    sentinel = jnp.int32(2 ** 30)
    k_io = lax.broadcasted_iota(jnp.int32, (PEER_TOPK, s.shape[1]), 0)
    vals = jnp.zeros((PEER_TOPK, s.shape[1]), F32)
    poss = jnp.zeros((PEER_TOPK, s.shape[1]), jnp.int32)
    pays = jnp.zeros((PEER_TOPK, s.shape[1]), jnp.int32)
    for it in range(PEER_TOPK):
        m = jnp.max(s, axis=0, keepdims=True)
        pos = jnp.min(jnp.where(s == m, pos_rows, sentinel), axis=0, keepdims=True)
        hit = pos_rows == pos
        vals = jnp.where(k_io == it, m, vals)
        poss = jnp.where(k_io == it, pos, poss)
        if payload is not None:
            pay = jnp.max(jnp.where(hit, payload, -1), axis=0, keepdims=True)
            pays = jnp.where(k_io == it, pay, pays)
        s = jnp.where(hit, -jnp.inf, s)
    return vals, poss, pays


def _pair_candidates(a, b, combine):
    rows = [combine(a[0:1], b)]
    rows += [combine(a[i:i + 1], b[0:SUBLANES]) for i in range(1, SUBLANES)]
    rows += [combine(a[SUBLANES:], b[0:1])]
    return jnp.concatenate(rows, axis=0)


def _pair_positions(n_tok):
    k = PEER_TOPK
    r = lax.broadcasted_iota(jnp.int32, (k + (SUBLANES - 1) * SUBLANES + SUBLANES, n_tok), 0)
    mid = (1 + lax.shift_right_logical(r - k, 3)) * k + lax.bitwise_and(r - k, SUBLANES - 1)
    tail = (r - (k + (SUBLANES - 1) * SUBLANES) + SUBLANES) * k
    return jnp.where(r < k, r, jnp.where(r < k + (SUBLANES - 1) * SUBLANES, mid, tail))


def _route_kernel(q_ref, keys_ref, w_ref, g_scr, i1_scr, i2_scr, gt_scr, i1t_scr, i2t_scr, wt_scr):
    nk = PEER_N_KEYS
    key_io = lax.broadcasted_iota(jnp.int32, (nk, ROUTE_TOK), 0)
    pair_pos = _pair_positions(ROUTE_TOK)

    def head(h, _):
        s1 = lax.dot_general(keys_ref[2 * h], q_ref[2 * h], _NT, preferred_element_type=F32)
        s2 = lax.dot_general(keys_ref[2 * h + 1], q_ref[2 * h + 1], _NT, preferred_element_type=F32)
        v1, p1, _ = _topk_rows(s1, key_io, None)
        v2, p2, _ = _topk_rows(s2, key_io, None)
        cand_s = _pair_candidates(v1, v2, lambda a, b: a + b)
        cand_i = _pair_candidates(p1, p2, lambda a, b: a * nk + b)
        best, _, idx = _topk_rows(cand_s, pair_pos, cand_i)
        e = jnp.exp(best - best[0:1])
        gate = e / jnp.sum(e, axis=0, keepdims=True)
        r0 = pl.multiple_of(h * PEER_TOPK, PEER_TOPK)
        g_scr[pl.ds(r0, PEER_TOPK), :] = gate
        i1_scr[pl.ds(r0, PEER_TOPK), :] = lax.shift_right_logical(idx, 7).astype(F32)
        i2_scr[pl.ds(r0, PEER_TOPK), :] = lax.bitwise_and(idx, nk - 1).astype(F32)
        return 0

    lax.fori_loop(0, PEER_HEADS, head, 0)

    gt_scr[...] = g_scr[...].T
    i1t_scr[...] = i1_scr[...].T
    i2t_scr[...] = i2_scr[...].T

    sub = lax.broadcasted_iota(jnp.int32, (nk, nk), 0).astype(F32)

    def token(t, _):
        g_row = gt_scr[pl.ds(t, 1), :]
        lhs = jnp.where(i1t_scr[pl.ds(t, 1), :] == sub, g_row, 0.0).astype(BF16)
        rhs = jnp.where(i2t_scr[pl.ds(t, 1), :] == sub, 1.0, 0.0).astype(BF16)
        w_t = lax.dot_general(lhs, rhs, _NT, preferred_element_type=F32)
        wt_scr[pl.ds(pl.multiple_of(t * W_ROW_STRIDE, W_ROW_STRIDE), nk), :] = w_t
        return 0

    lax.fori_loop(0, ROUTE_TOK, token, 0, unroll=8)

    def key_row(a, _):
        w_ref[a] = wt_scr[pl.ds(a, ROUTE_TOK, stride=W_ROW_STRIDE), :].astype(BF16)
        return 0

    lax.fori_loop(0, nk, key_row, 0, unroll=4)


def _route(q3, keys):
    nhp, rows, sd = q3.shape
    nk = PEER_N_KEYS
    return pl.pallas_call(
        _route_kernel,
        grid=(rows // ROUTE_TOK,),
        in_specs=[
            pl.BlockSpec((nhp, ROUTE_TOK, sd), lambda i: (0, i, 0)),
            pl.BlockSpec((nhp, nk, sd), lambda i: (0, 0, 0)),
        ],
        out_specs=pl.BlockSpec((nk, ROUTE_TOK, nk), lambda i: (0, i, 0)),
        out_shape=jax.ShapeDtypeStruct((nk, rows, nk), BF16),
        scratch_shapes=[pltpu.VMEM((PEER_HEADS * PEER_TOPK, ROUTE_TOK), F32)] * 3
        + [pltpu.VMEM((ROUTE_TOK, PEER_HEADS * PEER_TOPK), F32)] * 3
        + [pltpu.VMEM((ROUTE_TOK * W_ROW_STRIDE, nk), F32)],
        compiler_params=_params(("parallel",)),
        name="peer_route",
    )(q3, keys)


def _expert_kernel(hn_ref, u_ref, v_ref, w_ref, x1_ref, nw_ref, o_ref, acc_scr):
    j = pl.program_id(1)

    @pl.when(j == 0)
    def _():
        acc_scr[...] = x1_ref[...]

    act = lax.dot_general(hn_ref[...], u_ref[...], _NT, preferred_element_type=F32)
    gel = 0.5 * act * (1.0 + lax.erf(act * (2.0 ** -0.5)))
    wts = jnp.concatenate([w_ref[a] for a in range(w_ref.shape[0])], axis=1)
    coef = (gel * wts.astype(F32)).astype(BF16)
    acc_scr[...] += jnp.dot(coef, v_ref[...], preferred_element_type=F32)

    @pl.when(j == pl.num_programs(1) - 1)
    def _():
        y = acc_scr[...]
        ms = jnp.mean(y * y, axis=-1, keepdims=True)
        o_ref[...] = y * lax.rsqrt(ms + EPS) * nw_ref[...]


def _experts(hn, u, v, w3, x1, norm_w, tb, te):
    rows, d = hn.shape
    ne = u.shape[0]
    ka = te // PEER_N_KEYS
    return pl.pallas_call(
        _expert_kernel,
        grid=(rows // tb, ne // te),
        in_specs=[
            pl.BlockSpec((tb, d), lambda i, j: (i, 0)),
            pl.BlockSpec((te, d), lambda i, j: (j, 0)),
            pl.BlockSpec((te, d), lambda i, j: (j, 0)),
            pl.BlockSpec((ka, tb, PEER_N_KEYS), lambda i, j: (j, i, 0)),
            pl.BlockSpec((tb, d), lambda i, j: (i, 0)),
            pl.BlockSpec((1, d), lambda i, j: (0, 0)),
        ],
        out_specs=pl.BlockSpec((tb, d), lambda i, j: (i, 0)),
        out_shape=jax.ShapeDtypeStruct((rows, d), F32),
        scratch_shapes=[pltpu.VMEM((tb, d), F32)],
        compiler_params=_params(("parallel", "arbitrary")),
        name="peer_experts",
    )(hn, u, v, w3, x1, norm_w)


def _rope_tables(p_len):
    pos = jnp.maximum(jnp.arange(p_len) - PAD_LEN, 0).astype(F32)
    inv_freq = ROPE_THETA ** (-jnp.arange(0, ATTN_HEAD_DIM, 2, dtype=F32) / ATTN_HEAD_DIM)
    ang = pos[:, None] * inv_freq[None, :]
    cos = jnp.tile(jnp.cos(ang), (1, 4))
    sin = jnp.sin(ang)
    return cos, jnp.tile(jnp.concatenate([-sin, sin], axis=-1), (1, 2))


def _pick(n, candidates):
    for c in candidates:
        if n % c == 0:
            return c
    raise ValueError(f"no tile in {candidates} divides {n}")


def kernel(x, meta_tokens, norm_mix_w, w_in, attn_lambda_qk, attn_subln_w, mlstm_conv_w, mlstm_conv_b, mlstm_i_b, mlstm_f_b, mlstm_norm_w, w_out, norm_ffn_w, peer_w_q, peer_sub_keys, peer_u, peer_v, norm_final_w):
    batch, seq, d = x.shape
    assert w_in.shape[0] == 1, "single-layer block"
    p_len = seq + Q_BLOCK
    lambda_init = 0.8 - 0.6 * math.exp(-0.3 * 0)
    n_main = w_in.shape[-1] - 2 * MLSTM_HEADS

    head_rows = jnp.concatenate([jnp.zeros((PAD_LEN, d), x.dtype), meta_tokens.astype(x.dtype)], axis=0)
    hp = jnp.concatenate([jnp.broadcast_to(head_rows[None], (batch, Q_BLOCK, d)), x], axis=1)
    hp = hp.reshape(batch * p_len, d)

    w_main = w_in[0, :, :n_main].astype(BF16)
    w_gate = jnp.pad(w_in[0, :, n_main:], ((0, 0), (0, LANES - 2 * MLSTM_HEADS))).astype(BF16)
    zmain, zgate = _inproj(hp, norm_mix_w, w_main, w_gate,
                           tm=_pick(batch * p_len, (1024, 512, 256, 128)), tn=_pick(n_main, (1024, 512)))

    cos, sin = _rope_tables(p_len)
    attn = _attention(zmain, attn_lambda_qk[0], cos, sin, attn_subln_w, batch, seq, lambda_init,
                      tq=_pick(seq, (256, 128)))

    nc = p_len // MLSTM_CHUNK
    rows = -(-nc // 8) * 8
    gates = zgate[:, :2 * MLSTM_HEADS].reshape(batch, nc, MLSTM_CHUNK, 2, MLSTM_HEADS)
    gates = jnp.pad(gates.transpose(3, 0, 4, 1, 2), ((0, 0), (0, 0), (0, 0), (0, rows - nc), (0, 0)))
    gates = gates.reshape(2, batch * MLSTM_HEADS, rows, MLSTM_CHUNK)
    ml = _mlstm(zmain, gates[0], gates[1], mlstm_i_b[0], mlstm_f_b[0], mlstm_conv_w[0], mlstm_conv_b,
                mlstm_norm_w, batch, seq)

    half = ATTN_HEADS * ATTN_V_DIM
    x1, hn, q3 = _outproj(attn.reshape(batch * seq, half), ml.reshape(batch * seq, d - half),
                          x.reshape(batch * seq, d), w_out[0, :half].astype(BF16), w_out[0, half:].astype(BF16),
                          norm_ffn_w, peer_w_q[0].astype(BF16), tm=_pick(batch * seq, (512, 256, 128)))

    keys = peer_sub_keys[0].reshape(2 * PEER_HEADS, PEER_N_KEYS, PEER_SUB_DIM).astype(BF16)
    w3 = _route(q3, keys)

    out = _experts(hn, peer_u[0].astype(BF16), peer_v[0].astype(BF16), w3, x1, norm_final_w[None],
                   tb=_pick(batch * seq, (512, 256, 128)), te=512)
    return out.reshape(batch, seq, d)
```

```python
import functools
import math

import jax
import jax.numpy as jnp
from jax import lax
from jax.experimental import pallas as pl
from jax.experimental.pallas import tpu as pltpu

F32 = jnp.float32
BF16 = jnp.bfloat16

N_META = 16
Q_BLOCK = 128
PAD_LEN = Q_BLOCK - N_META
ATTN_HEADS = 8
ATTN_HEAD_DIM = 64
ATTN_V_DIM = 128
ROPE_THETA = 10000.0
MLSTM_HEADS = 4
MLSTM_QK_DIM = 128
MLSTM_V_DIM = 256
MLSTM_CHUNK = 64
CONV_WIDTH = 4
PEER_HEADS = 8
PEER_N_KEYS = 128
PEER_SUB_DIM = 128
PEER_TOPK = 16
EPS = 1e-6
NEG = -1e30

LANES = 128
SUBLANES = 8
VMEM_LIMIT = 56 * 1024 * 1024

_NT = (((1,), (1,)), ((), ()))
_TN = (((0,), (0,)), ((), ()))


def _params(sem):
    return pltpu.CompilerParams(dimension_semantics=sem, vmem_limit_bytes=VMEM_LIMIT)


def _inproj_kernel(x_ref, nw_ref, w_ref, wg_ref, z_ref, zg_ref, h_scr):
    @pl.when(pl.program_id(1) == 0)
    def _():
        x = x_ref[...]
        ms = jnp.mean(x * x, axis=-1, keepdims=True)
        h_scr[...] = (x * lax.rsqrt(ms + EPS) * nw_ref[...]).astype(BF16)
        zg_ref[...] = jnp.dot(h_scr[...], wg_ref[...], preferred_element_type=F32)

    z_ref[...] = jnp.dot(h_scr[...], w_ref[...], preferred_element_type=F32).astype(BF16)


def _inproj(hp, norm_w, w_main, w_gate, tm, tn):
    rows, d = hp.shape
    cols = w_main.shape[1]
    return pl.pallas_call(
        _inproj_kernel,
        grid=(rows // tm, cols // tn),
        in_specs=[
            pl.BlockSpec((tm, d), lambda i, j: (i, 0)),
            pl.BlockSpec((1, d), lambda i, j: (0, 0)),
            pl.BlockSpec((d, tn), lambda i, j: (0, j)),
            pl.BlockSpec((d, LANES), lambda i, j: (0, 0)),
        ],
        out_specs=[
            pl.BlockSpec((tm, tn), lambda i, j: (i, j)),
            pl.BlockSpec((tm, LANES), lambda i, j: (i, 0)),
        ],
        out_shape=[
            jax.ShapeDtypeStruct((rows, cols), BF16),
            jax.ShapeDtypeStruct((rows, LANES), F32),
        ],
        scratch_shapes=[pltpu.VMEM((tm, d), BF16)],
        compiler_params=_params(("parallel", "arbitrary")),
        name="inproj",
    )(hp, norm_w, w_main, w_gate)


def _attn_kernel(lq_ref, q_ref, k_ref, v_ref, cos_ref, sin_ref, sw_ref, o_ref,
                 q1_scr, q2_scr, k_scr, *, seq, tq, lambda_init):
    p_len = seq + Q_BLOCK
    lane = lax.broadcasted_iota(jnp.int32, (p_len, LANES), 1)
    low_half = (lane % ATTN_HEAD_DIM) < (ATTN_HEAD_DIM // 2)
    first_map = lane < ATTN_HEAD_DIM

    def rope(x):
        rot = jnp.where(low_half, pltpu.roll(x, LANES - 32, 1), pltpu.roll(x, 32, 1))
        return x * cos_ref[...] + rot * sin_ref[...]

    q = rope(q_ref[0].astype(F32)) * (ATTN_HEAD_DIM ** -0.5)
    q1_scr[...] = jnp.where(first_map, q, 0.0).astype(BF16)
    q2_scr[...] = jnp.where(first_map, 0.0, q).astype(BF16)
    k_scr[...] = rope(k_ref[0].astype(F32)).astype(BF16)

    lq = lq_ref[...]
    lam = (jnp.exp(jnp.sum(lq[0:1] * lq[1:2], axis=-1, keepdims=True))
           - jnp.exp(jnp.sum(lq[2:3] * lq[3:4], axis=-1, keepdims=True)) + lambda_init)

    def step(q1, q2, kt, vt, carry, mask):
        m1, l1, a1, m2, l2, a2 = carry
        s1 = lax.dot_general(q1, kt, _NT, preferred_element_type=F32)
        s2 = lax.dot_general(q2, kt, _NT, preferred_element_type=F32)
        if mask is not None:
            s1 = jnp.where(mask, s1, NEG)
            s2 = jnp.where(mask, s2, NEG)
        n1 = jnp.maximum(m1, jnp.max(s1, axis=-1, keepdims=True))
        n2 = jnp.maximum(m2, jnp.max(s2, axis=-1, keepdims=True))
        p1 = jnp.exp(s1 - n1)
        p2 = jnp.exp(s2 - n2)
        c1 = jnp.exp(m1 - n1)
        c2 = jnp.exp(m2 - n2)
        l1 = c1 * l1 + jnp.sum(p1, axis=-1, keepdims=True)
        l2 = c2 * l2 + jnp.sum(p2, axis=-1, keepdims=True)
        a1 = c1 * a1 + jnp.dot(p1.astype(BF16), vt, preferred_element_type=F32)
        a2 = c2 * a2 + jnp.dot(p2.astype(BF16), vt, preferred_element_type=F32)
        return n1, l1, a1, n2, l2, a2

    meta_mask = lax.broadcasted_iota(jnp.int32, (tq, Q_BLOCK), 1) >= PAD_LEN
    diag_mask = (lax.broadcasted_iota(jnp.int32, (tq, tq), 1)
                 <= lax.broadcasted_iota(jnp.int32, (tq, tq), 0))

    def q_tile(i, _):
        r0 = pl.multiple_of(Q_BLOCK + i * tq, Q_BLOCK)
        q1 = q1_scr[pl.ds(r0, tq), :]
        q2 = q2_scr[pl.ds(r0, tq), :]
        neg = jnp.full((tq, 1), NEG, F32)
        zero = jnp.zeros((tq, 1), F32)
        zacc = jnp.zeros((tq, ATTN_V_DIM), F32)
        carry = (neg, zero, zacc, neg, zero, zacc)
        carry = step(q1, q2, k_scr[0:Q_BLOCK, :], v_ref[0, 0:Q_BLOCK, :], carry, meta_mask)

        def full(j, c):
            k0 = pl.multiple_of(Q_BLOCK + j * tq, Q_BLOCK)
            return step(q1, q2, k_scr[pl.ds(k0, tq), :], v_ref[0, pl.ds(k0, tq), :], c, None)

        carry = lax.fori_loop(0, i, full, carry)
        carry = step(q1, q2, k_scr[pl.ds(r0, tq), :], v_ref[0, pl.ds(r0, tq), :], carry, diag_mask)
        _, l1, a1, _, l2, a2 = carry
        o = a1 / l1 - lam * (a2 / l2)
        ms = jnp.mean(o * o, axis=-1, keepdims=True)
        o = o * lax.rsqrt(ms + EPS) * sw_ref[...] * (1.0 - lambda_init)
        o_ref[0, pl.ds(pl.multiple_of(i * tq, tq), tq), :] = o.astype(BF16)
        return 0

    lax.fori_loop(0, seq // tq, q_tile, 0)


def _attention(zmain, lam_qk, cos, sin, subln_w, batch, seq, lambda_init, tq):
    p_len = seq + Q_BLOCK
    z3 = zmain.reshape(batch, p_len, zmain.shape[-1])
    kern = functools.partial(_attn_kernel, seq=seq, tq=tq, lambda_init=lambda_init)
    blk = lambda off: pl.BlockSpec((1, p_len, LANES), lambda b, h, off=off: (b, 0, off + h))
    const = lambda shape: pl.BlockSpec(shape, lambda b, h: (0,) * len(shape))
    return pl.pallas_call(
        kern,
        grid=(batch, ATTN_HEADS),
        in_specs=[
            const((4, ATTN_HEAD_DIM)),
            blk(0), blk(ATTN_HEADS), blk(2 * ATTN_HEADS),
            const((p_len, LANES)), const((p_len, LANES)), const((1, ATTN_V_DIM)),
        ],
        out_specs=pl.BlockSpec((1, seq, ATTN_V_DIM), lambda b, h: (b, 0, h)),
        out_shape=jax.ShapeDtypeStruct((batch, seq, ATTN_HEADS * ATTN_V_DIM), BF16),
        scratch_shapes=[pltpu.VMEM((p_len, LANES), BF16)] * 3,
        compiler_params=_params(("parallel", "parallel")),
        name="diff_attn",
    )(lam_qk, z3, z3, z3, cos, sin, subln_w)


def _mlstm_kernel(ib_ref, fb_ref, q_ref, k_ref, v_ref, og_ref, cwq_ref, cwk_ref, cbq_ref, cbk_ref,
                  gi_ref, gf_ref, nw_ref, o_ref,
                  pad_scr, q_scr, k_scr, b_scr, li_scr, c_scr, *, seq):
    p_len = seq + Q_BLOCK
    nc = p_len // MLSTM_CHUNK
    cl = MLSTM_CHUNK
    nh = MLSTM_HEADS
    dk, dv = MLSTM_QK_DIM, MLSTM_V_DIM

    def conv_silu(x_ref, w_ref, b_ref, h):
        cols = slice(h * dk, (h + 1) * dk)
        pad_scr[0:8, :] = jnp.zeros((8, dk), F32)
        pad_scr[8:8 + p_len, :] = x_ref[0, :, cols].astype(F32)
        y = b_ref[:, cols] + w_ref[3:4, cols] * pad_scr[8:8 + p_len, :]
        for j in range(CONV_WIDTH - 1):
            y = y + w_ref[j:j + 1, cols] * pad_scr[5 + j:5 + j + p_len, :]
        return y * (1.0 / (1.0 + jnp.exp(-y)))

    rows = gi_ref.shape[1]
    pos = (lax.broadcasted_iota(jnp.int32, (rows, cl), 0) * cl
           + lax.broadcasted_iota(jnp.int32, (rows, cl), 1))
    valid = pos >= PAD_LEN
    upper = (lax.broadcasted_iota(jnp.int32, (cl, cl), 0)
             <= lax.broadcasted_iota(jnp.int32, (cl, cl), 1)).astype(F32)
    for h in range(nh):
        q_scr[:, h * dk:(h + 1) * dk] = (conv_silu(q_ref, cwq_ref, cbq_ref, h) * (dk ** -0.5)).astype(BF16)
        k_scr[:, h * dk:(h + 1) * dk] = conv_silu(k_ref, cwk_ref, cbk_ref, h)
        fpre = gf_ref[h] + fb_ref[h]
        log_f = jnp.minimum(fpre, 0.0) - jnp.log(1.0 + jnp.exp(-jnp.abs(fpre)))
        log_f = jnp.where(valid, log_f, 0.0)
        li_scr[h] = jnp.where(valid, gi_ref[h] + ib_ref[h], NEG)
        b_scr[h] = jnp.dot(log_f, upper, preferred_element_type=F32,
                           precision=lax.Precision.HIGHEST)

    c_scr[...] = jnp.zeros_like(c_scr)
    r_io = lax.broadcasted_iota(jnp.int32, (cl, cl), 0)
    c_io = lax.broadcasted_iota(jnp.int32, (cl, cl), 1)
    eye = r_io == c_io
    tril = c_io <= r_io

    def to_col(row):
        return jnp.sum(jnp.where(eye, row, 0.0), axis=1, keepdims=True)

    def head_chunk(h, c, r0, m, n):
        b_r = b_scr[h, pl.ds(c, 1), :]
        li_r = li_scr[h, pl.ds(c, 1), :]
        g = b_r[:, cl - 1:cl]
        a_r = g - b_r + li_r
        b_c = to_col(b_r)
        a_c = to_col(a_r)
        dmat = jnp.where(tril, b_c - b_r + li_r, NEG)
        m_inter = b_c + m
        m_t = jnp.maximum(jnp.max(dmat, axis=1, keepdims=True), m_inter)
        qc = q_scr[pl.ds(r0, cl), h * dk:(h + 1) * dk]
        kc = k_scr[pl.ds(r0, cl), h * dk:(h + 1) * dk]
        vc = v_ref[0, pl.ds(r0, cl), h * dv:(h + 1) * dv]
        s = lax.dot_general(qc, kc.astype(BF16), _NT, preferred_element_type=F32) * jnp.exp(dmat - m_t)
        inter_w = jnp.exp(m_inter - m_t)
        num = (jnp.dot(s.astype(BF16), vc, preferred_element_type=F32)
               + inter_w * jnp.dot(qc, c_scr[h].astype(BF16), preferred_element_type=F32))
        nq = (jnp.sum(s, axis=1, keepdims=True)
              + inter_w * jnp.sum(qc.astype(F32) * n, axis=1, keepdims=True))
        hid = num / jnp.maximum(jnp.abs(nq), jnp.exp(-m_t))
        ms = jnp.mean(hid * hid, axis=-1, keepdims=True)
        hn = hid * lax.rsqrt(ms + EPS) * nw_ref[:, h * dv:(h + 1) * dv]
        og = og_ref[0, pl.ds(r0, cl), h * dv:(h + 1) * dv].astype(F32)
        out = hn * (1.0 / (1.0 + jnp.exp(-og)))
        o0 = pl.multiple_of(jnp.maximum(r0 - Q_BLOCK, 0), cl)
        o_ref[0, pl.ds(o0, cl), h * dv:(h + 1) * dv] = out.astype(BF16)

        m_new = jnp.maximum(g + m, jnp.max(a_r, axis=1, keepdims=True))
        w_c = jnp.exp(a_c - m_new)
        decay = jnp.exp(g + m - m_new)
        kw = kc * w_c
        c_scr[h] = decay * c_scr[h] + lax.dot_general(kw.astype(BF16), vc, _TN, preferred_element_type=F32)
        n_new = decay * n + jnp.sum(kw, axis=0, keepdims=True)
        return m_new, n_new

    def chunk(c, carry):
        r0 = pl.multiple_of(c * cl, cl)
        return tuple(head_chunk(h, c, r0, *carry[h]) for h in range(nh))

    init = tuple((jnp.full((1, 1), NEG, F32), jnp.zeros((1, dk), F32)) for _ in range(nh))
    lax.fori_loop(0, nc, chunk, init)


def _mlstm(zmain, gi_rows, gf_rows, i_b, f_b, conv_w, conv_b, norm_w, batch, seq):
    p_len = seq + Q_BLOCK
    z3 = zmain.reshape(batch, p_len, zmain.shape[-1])
    nh = MLSTM_HEADS
    rows = gi_rows.shape[1]
    qk_w = nh * MLSTM_QK_DIM
    v_w = nh * MLSTM_V_DIM
    q_off = 3 * ATTN_HEADS * LANES // qk_w
    v_off = (q_off + 2) * qk_w // v_w
    smem = pl.BlockSpec(memory_space=pltpu.SMEM)
    return pl.pallas_call(
        functools.partial(_mlstm_kernel, seq=seq),
        grid=(batch,),
        in_specs=[
            smem, smem,
            pl.BlockSpec((1, p_len, qk_w), lambda b: (b, 0, q_off)),
            pl.BlockSpec((1, p_len, qk_w), lambda b: (b, 0, q_off + 1)),
            pl.BlockSpec((1, p_len, v_w), lambda b: (b, 0, v_off)),
            pl.BlockSpec((1, p_len, v_w), lambda b: (b, 0, v_off + 1)),
            pl.BlockSpec((CONV_WIDTH, qk_w), lambda b: (0, 0)),
            pl.BlockSpec((CONV_WIDTH, qk_w), lambda b: (0, 1)),
            pl.BlockSpec((1, qk_w), lambda b: (0, 0)),
            pl.BlockSpec((1, qk_w), lambda b: (0, 1)),
            pl.BlockSpec((nh, rows, MLSTM_CHUNK), lambda b: (b, 0, 0)),
            pl.BlockSpec((nh, rows, MLSTM_CHUNK), lambda b: (b, 0, 0)),
            pl.BlockSpec((1, v_w), lambda b: (0, 0)),
        ],
        out_specs=pl.BlockSpec((1, seq, v_w), lambda b: (b, 0, 0)),
        out_shape=jax.ShapeDtypeStruct((batch, seq, v_w), BF16),
        scratch_shapes=[
            pltpu.VMEM((p_len + 8, MLSTM_QK_DIM), F32),
            pltpu.VMEM((p_len, qk_w), BF16),
            pltpu.VMEM((p_len, qk_w), F32),
            pltpu.VMEM((nh, rows, MLSTM_CHUNK), F32),
            pltpu.VMEM((nh, rows, MLSTM_CHUNK), F32),
            pltpu.VMEM((nh, MLSTM_QK_DIM, MLSTM_V_DIM), F32),
        ],
        compiler_params=_params(("parallel",)),
        name="mlstm",
    )(i_b, f_b, z3, z3, z3, z3, conv_w, conv_w, conv_b, conv_b, gi_rows, gf_rows, norm_w)


def _outproj_kernel(a_ref, m_ref, x_ref, wa_ref, wm_ref, nw_ref, wq_ref, x1_ref, hn_ref, q_ref):
    x1 = (x_ref[...]
          + jnp.dot(a_ref[...], wa_ref[...], preferred_element_type=F32)
          + jnp.dot(m_ref[...], wm_ref[...], preferred_element_type=F32))
    x1_ref[...] = x1
    ms = jnp.mean(x1 * x1, axis=-1, keepdims=True)
    hn = (x1 * lax.rsqrt(ms + EPS) * nw_ref[...]).astype(BF16)
    hn_ref[...] = hn
    q = jnp.dot(hn, wq_ref[...], preferred_element_type=F32).astype(BF16)
    for hp in range(2 * PEER_HEADS):
        q_ref[hp] = q[:, hp * PEER_SUB_DIM:(hp + 1) * PEER_SUB_DIM]


def _outproj(attn, ml, x2, w_attn, w_ml, norm_w, w_q, tm):
    rows, d = x2.shape
    half = attn.shape[1]
    single = pl.Buffered(1)
    return pl.pallas_call(
        _outproj_kernel,
        grid=(rows // tm,),
        in_specs=[
            pl.BlockSpec((tm, half), lambda i: (i, 0)),
            pl.BlockSpec((tm, half), lambda i: (i, 0)),
            pl.BlockSpec((tm, d), lambda i: (i, 0)),
            pl.BlockSpec((half, d), lambda i: (0, 0), pipeline_mode=single),
            pl.BlockSpec((half, d), lambda i: (0, 0), pipeline_mode=single),
            pl.BlockSpec((1, d), lambda i: (0, 0)),
            pl.BlockSpec((d, d), lambda i: (0, 0), pipeline_mode=single),
        ],
        out_specs=[
            pl.BlockSpec((tm, d), lambda i: (i, 0)),
            pl.BlockSpec((tm, d), lambda i: (i, 0)),
            pl.BlockSpec((2 * PEER_HEADS, tm, PEER_SUB_DIM), lambda i: (0, i, 0)),
        ],
        out_shape=[
            jax.ShapeDtypeStruct((rows, d), F32),
            jax.ShapeDtypeStruct((rows, d), BF16),
            jax.ShapeDtypeStruct((2 * PEER_HEADS, rows, PEER_SUB_DIM), BF16),
        ],
        compiler_params=_params(("parallel",)),
        name="outproj",
    )(attn, ml, x2, w_attn, w_ml, norm_w, w_q)


ROUTE_TOK = LANES
W_ROW_STRIDE = PEER_N_KEYS + 8


def _topk_rows(s, pos_rows, payload):
    sentinel = jnp.int32(2 ** 30)
    k_io = lax.broadcasted_iota(jnp.int32, (PEER_TOPK, s.shape[1]), 0)
    vals = jnp.zeros((PEER_TOPK, s.shape[1]), F32)
    poss = jnp.zeros((PEER_TOPK, s.shape[1]), jnp.int32)
    pays = jnp.zeros((PEER_TOPK, s.shape[1]), jnp.int32)
    for it in range(PEER_TOPK):
        m = jnp.max(s, axis=0, keepdims=True)
        pos = jnp.min(jnp.where(s == m, pos_rows, sentinel), axis=0, keepdims=True)
        hit = pos_rows == pos
        vals = jnp.where(k_io == it, m, vals)
        poss = jnp.where(k_io == it, pos, poss)
        if payload is not None:
            pay = jnp.max(jnp.where(hit, payload, -1), axis=0, keepdims=True)
            pays = jnp.where(k_io == it, pay, pays)
        s = jnp.where(hit, -jnp.inf, s)
    return vals, poss, pays


def _pair_candidates(a, b, combine):
    rows = [combine(a[0:1], b)]
    rows += [combine(a[i:i + 1], b[0:SUBLANES]) for i in range(1, SUBLANES)]
    rows += [combine(a[SUBLANES:], b[0:1])]
    return jnp.concatenate(rows, axis=0)


def _pair_positions(n_tok):
    k = PEER_TOPK
    r = lax.broadcasted_iota(jnp.int32, (k + (SUBLANES - 1) * SUBLANES + SUBLANES, n_tok), 0)
    mid = (1 + lax.shift_right_logical(r - k, 3)) * k + lax.bitwise_and(r - k, SUBLANES - 1)
    tail = (r - (k + (SUBLANES - 1) * SUBLANES) + SUBLANES) * k
    return jnp.where(r < k, r, jnp.where(r < k + (SUBLANES - 1) * SUBLANES, mid, tail))


def _route_kernel(q_ref, keys_ref, w_ref, g_scr, i1_scr, i2_scr, gt_scr, i1t_scr, i2t_scr, wt_scr):
    nk = PEER_N_KEYS
    key_io = lax.broadcasted_iota(jnp.int32, (nk, ROUTE_TOK), 0)
    pair_pos = _pair_positions(ROUTE_TOK)

    def stage1(h):
        s1 = lax.dot_general(keys_ref[2 * h], q_ref[2 * h], _NT, preferred_element_type=F32)
        s2 = lax.dot_general(keys_ref[2 * h + 1], q_ref[2 * h + 1], _NT, preferred_element_type=F32)
        v1, p1, _ = _topk_rows(s1, key_io, None)
        v2, p2, _ = _topk_rows(s2, key_io, None)
        return v1, p1, v2, p2

    def stage2(h, tops):
        v1, p1, v2, p2 = tops
        cand_s = _pair_candidates(v1, v2, lambda a, b: a + b)
        cand_i = _pair_candidates(p1, p2, lambda a, b: a * nk + b)
        best, _, idx = _topk_rows(cand_s, pair_pos, cand_i)
        e = jnp.exp(best - best[0:1])
        gate = e / jnp.sum(e, axis=0, keepdims=True)
        r0 = pl.multiple_of(h * PEER_TOPK, PEER_TOPK)
        g_scr[pl.ds(r0, PEER_TOPK), :] = gate
        i1_scr[pl.ds(r0, PEER_TOPK), :] = lax.shift_right_logical(idx, 7).astype(F32)
        i2_scr[pl.ds(r0, PEER_TOPK), :] = lax.bitwise_and(idx, nk - 1).astype(F32)

    def head(h, tops):
        nxt = stage1(h + 1)
        stage2(h, tops)
        return nxt

    stage2(PEER_HEADS - 1, lax.fori_loop(0, PEER_HEADS - 1, head, stage1(0)))

    gt_scr[...] = g_scr[...].T
    i1t_scr[...] = i1_scr[...].T
    i2t_scr[...] = i2_scr[...].T

    sub = lax.broadcasted_iota(jnp.int32, (nk, nk), 0).astype(F32)

    def token(t, _):
        g_row = gt_scr[pl.ds(t, 1), :]
        lhs = jnp.where(i1t_scr[pl.ds(t, 1), :] == sub, g_row, 0.0).astype(BF16)
        rhs = jnp.where(i2t_scr[pl.ds(t, 1), :] == sub, 1.0, 0.0).astype(BF16)
        w_t = lax.dot_general(lhs, rhs, _NT, preferred_element_type=F32)
        wt_scr[pl.ds(pl.multiple_of(t * W_ROW_STRIDE, W_ROW_STRIDE), nk), :] = w_t
        return 0

    lax.fori_loop(0, ROUTE_TOK, token, 0, unroll=16)

    def key_row(a, _):
        w_ref[a] = wt_scr[pl.ds(a, ROUTE_TOK, stride=W_ROW_STRIDE), :].astype(BF16)
        return 0

    lax.fori_loop(0, nk, key_row, 0, unroll=4)


def _route(q3, keys):
    nhp, rows, sd = q3.shape
    nk = PEER_N_KEYS
    return pl.pallas_call(
        _route_kernel,
        grid=(rows // ROUTE_TOK,),
        in_specs=[
            pl.BlockSpec((nhp, ROUTE_TOK, sd), lambda i: (0, i, 0)),
            pl.BlockSpec((nhp, nk, sd), lambda i: (0, 0, 0)),
        ],
        out_specs=pl.BlockSpec((nk, ROUTE_TOK, nk), lambda i: (0, i, 0)),
        out_shape=jax.ShapeDtypeStruct((nk, rows, nk), BF16),
        scratch_shapes=[pltpu.VMEM((PEER_HEADS * PEER_TOPK, ROUTE_TOK), F32)] * 3
        + [pltpu.VMEM((ROUTE_TOK, PEER_HEADS * PEER_TOPK), F32)] * 3
        + [pltpu.VMEM((ROUTE_TOK * W_ROW_STRIDE, nk), F32)],
        compiler_params=_params(("parallel",)),
        name="peer_route",
    )(q3, keys)


def _expert_kernel(hn_ref, u_ref, v_ref, w_ref, x1_ref, nw_ref, o_ref):
    j = pl.program_id(1)

    @pl.when(j == 0)
    def _():
        o_ref[...] = x1_ref[...]

    act = lax.dot_general(hn_ref[...], u_ref[...], _NT, preferred_element_type=F32)
    gel = 0.5 * act * (1.0 + lax.erf(act * (2.0 ** -0.5)))
    wts = jnp.concatenate([w_ref[a] for a in range(w_ref.shape[0])], axis=1)
    coef = (gel * wts.astype(F32)).astype(BF16)
    o_ref[...] += jnp.dot(coef, v_ref[...], preferred_element_type=F32)

    @pl.when(j == pl.num_programs(1) - 1)
    def _():
        y = o_ref[...]
        ms = jnp.mean(y * y, axis=-1, keepdims=True)
        o_ref[...] = y * lax.rsqrt(ms + EPS) * nw_ref[...]


def _experts(hn, u, v, w3, x1, norm_w, tb, te):
    rows, d = hn.shape
    ne = u.shape[0]
    ka = te // PEER_N_KEYS
    return pl.pallas_call(
        _expert_kernel,
        grid=(rows // tb, ne // te),
        in_specs=[
            pl.BlockSpec((tb, d), lambda i, j: (i, 0)),
            pl.BlockSpec((te, d), lambda i, j: (j, 0)),
            pl.BlockSpec((te, d), lambda i, j: (j, 0)),
            pl.BlockSpec((ka, tb, PEER_N_KEYS), lambda i, j: (j, i, 0)),
            pl.BlockSpec((tb, d), lambda i, j: (i, 0), pipeline_mode=pl.Buffered(1)),
            pl.BlockSpec((1, d), lambda i, j: (0, 0)),
        ],
        out_specs=pl.BlockSpec((tb, d), lambda i, j: (i, 0)),
        out_shape=jax.ShapeDtypeStruct((rows, d), F32),
        compiler_params=_params(("parallel", "arbitrary")),
        name="peer_experts",
    )(hn, u, v, w3, x1, norm_w)


def _rope_tables(p_len):
    pos = jnp.maximum(jnp.arange(p_len) - PAD_LEN, 0).astype(F32)
    inv_freq = ROPE_THETA ** (-jnp.arange(0, ATTN_HEAD_DIM, 2, dtype=F32) / ATTN_HEAD_DIM)
    ang = pos[:, None] * inv_freq[None, :]
    cos = jnp.tile(jnp.cos(ang), (1, 4))
    sin = jnp.sin(ang)
    return cos, jnp.tile(jnp.concatenate([-sin, sin], axis=-1), (1, 2))


def _pick(n, candidates):
    for c in candidates:
        if n % c == 0:
            return c
    raise ValueError(f"no tile in {candidates} divides {n}")


def kernel(x, meta_tokens, norm_mix_w, w_in, attn_lambda_qk, attn_subln_w, mlstm_conv_w, mlstm_conv_b, mlstm_i_b, mlstm_f_b, mlstm_norm_w, w_out, norm_ffn_w, peer_w_q, peer_sub_keys, peer_u, peer_v, norm_final_w):
    batch, seq, d = x.shape
    assert w_in.shape[0] == 1, "single-layer block"
    p_len = seq + Q_BLOCK
    lambda_init = 0.8 - 0.6 * math.exp(-0.3 * 0)
    n_main = w_in.shape[-1] - 2 * MLSTM_HEADS

    head_rows = jnp.concatenate([jnp.zeros((PAD_LEN, d), x.dtype), meta_tokens.astype(x.dtype)], axis=0)
    hp = jnp.concatenate([jnp.broadcast_to(head_rows[None], (batch, Q_BLOCK, d)), x], axis=1)
    hp = hp.reshape(batch * p_len, d)

    w_main = w_in[0, :, :n_main].astype(BF16)
    w_gate = jnp.pad(w_in[0, :, n_main:], ((0, 0), (0, LANES - 2 * MLSTM_HEADS))).astype(BF16)
    zmain, zgate = _inproj(hp, norm_mix_w, w_main, w_gate,
                           tm=_pick(batch * p_len, (1024, 512, 256, 128)), tn=_pick(n_main, (1024, 512)))

    cos, sin = _rope_tables(p_len)
    attn = _attention(zmain, attn_lambda_qk[0], cos, sin, attn_subln_w, batch, seq, lambda_init,
                      tq=_pick(seq, (512, 256, 128)))

    nc = p_len // MLSTM_CHUNK
    rows = -(-nc // 8) * 8
    gates = zgate[:, :2 * MLSTM_HEADS].reshape(batch, nc, MLSTM_CHUNK, 2, MLSTM_HEADS)
    gates = jnp.pad(gates.transpose(3, 0, 4, 1, 2), ((0, 0), (0, 0), (0, 0), (0, rows - nc), (0, 0)))
    gates = gates.reshape(2, batch * MLSTM_HEADS, rows, MLSTM_CHUNK)
    ml = _mlstm(zmain, gates[0], gates[1], mlstm_i_b[0], mlstm_f_b[0], mlstm_conv_w[0], mlstm_conv_b,
                mlstm_norm_w, batch, seq)

    half = ATTN_HEADS * ATTN_V_DIM
    x1, hn, q3 = _outproj(attn.reshape(batch * seq, half), ml.reshape(batch * seq, d - half),
                          x.reshape(batch * seq, d), w_out[0, :half].astype(BF16), w_out[0, half:].astype(BF16),
                          norm_ffn_w, peer_w_q[0].astype(BF16), tm=_pick(batch * seq, (512, 256, 128)))

    keys = peer_sub_keys[0].reshape(2 * PEER_HEADS, PEER_N_KEYS, PEER_SUB_DIM).astype(BF16)
    w3 = _route(q3, keys)

    out = _experts(hn, peer_u[0].astype(BF16), peer_v[0].astype(BF16), w3, x1, norm_final_w[None],
                   tb=_pick(batch * seq, (1024, 512, 256, 128)), te=512)
    return out.reshape(batch, seq, d)
```

```python
import functools
import math

import jax
import jax.numpy as jnp
from jax import lax
from jax.experimental import pallas as pl
from jax.experimental.pallas import tpu as pltpu

F32 = jnp.float32
BF16 = jnp.bfloat16

N_META = 16
Q_BLOCK = 128
PAD_LEN = Q_BLOCK - N_META
ATTN_HEADS = 8
ATTN_HEAD_DIM = 64
ATTN_V_DIM = 128
ROPE_THETA = 10000.0
MLSTM_HEADS = 4
MLSTM_QK_DIM = 128
MLSTM_V_DIM = 256
MLSTM_CHUNK = 64
CONV_WIDTH = 4
PEER_HEADS = 8
PEER_N_KEYS = 128
PEER_SUB_DIM = 128
PEER_TOPK = 16
EPS = 1e-6
NEG = -1e30

LANES = 128
SUBLANES = 8
VMEM_LIMIT = 56 * 1024 * 1024

_NT = (((1,), (1,)), ((), ()))
_TN = (((0,), (0,)), ((), ()))


def _params(sem):
    return pltpu.CompilerParams(dimension_semantics=sem, vmem_limit_bytes=VMEM_LIMIT)


def _inproj_kernel(x_ref, nw_ref, w_ref, wg_ref, z_ref, zg_ref, h_scr):
    @pl.when(pl.program_id(1) == 0)
    def _():
        x = x_ref[...]
        ms = jnp.mean(x * x, axis=-1, keepdims=True)
        h_scr[...] = (x * lax.rsqrt(ms + EPS) * nw_ref[...]).astype(BF16)
        zg_ref[...] = jnp.dot(h_scr[...], wg_ref[...], preferred_element_type=F32)

    z_ref[...] = jnp.dot(h_scr[...], w_ref[...], preferred_element_type=F32).astype(BF16)


def _inproj(hp, norm_w, w_main, w_gate, tm, tn):
    rows, d = hp.shape
    cols = w_main.shape[1]
    return pl.pallas_call(
        _inproj_kernel,
        grid=(rows // tm, cols // tn),
        in_specs=[
            pl.BlockSpec((tm, d), lambda i, j: (i, 0)),
            pl.BlockSpec((1, d), lambda i, j: (0, 0)),
            pl.BlockSpec((d, tn), lambda i, j: (0, j)),
            pl.BlockSpec((d, LANES), lambda i, j: (0, 0)),
        ],
        out_specs=[
            pl.BlockSpec((tm, tn), lambda i, j: (i, j)),
            pl.BlockSpec((tm, LANES), lambda i, j: (i, 0)),
        ],
        out_shape=[
            jax.ShapeDtypeStruct((rows, cols), BF16),
            jax.ShapeDtypeStruct((rows, LANES), F32),
        ],
        scratch_shapes=[pltpu.VMEM((tm, d), BF16)],
        compiler_params=_params(("parallel", "arbitrary")),
        name="inproj",
    )(hp, norm_w, w_main, w_gate)


def _attn_kernel(lq_ref, q_ref, k_ref, v_ref, cos_ref, sin_ref, sw_ref, o_ref,
                 q1_scr, q2_scr, k_scr, *, seq, tq, lambda_init):
    p_len = seq + Q_BLOCK
    lane = lax.broadcasted_iota(jnp.int32, (p_len, LANES), 1)
    low_half = (lane % ATTN_HEAD_DIM) < (ATTN_HEAD_DIM // 2)
    first_map = lane < ATTN_HEAD_DIM

    def rope(x):
        rot = jnp.where(low_half, pltpu.roll(x, LANES - 32, 1), pltpu.roll(x, 32, 1))
        return x * cos_ref[...] + rot * sin_ref[...]

    q = rope(q_ref[0].astype(F32)) * (ATTN_HEAD_DIM ** -0.5)
    q1_scr[...] = jnp.where(first_map, q, 0.0).astype(BF16)
    q2_scr[...] = jnp.where(first_map, 0.0, q).astype(BF16)
    k_scr[...] = rope(k_ref[0].astype(F32)).astype(BF16)

    lq = lq_ref[...]
    lam = (jnp.exp(jnp.sum(lq[0:1] * lq[1:2], axis=-1, keepdims=True))
           - jnp.exp(jnp.sum(lq[2:3] * lq[3:4], axis=-1, keepdims=True)) + lambda_init)

    def step(q12, kt, vt, carry, mask):
        m, l, acc = carry
        s = lax.dot_general(q12, kt, _NT, preferred_element_type=F32)
        if mask is not None:
            s = jnp.where(mask, s, NEG)
        n = jnp.maximum(m, jnp.max(s, axis=-1, keepdims=True))
        p = jnp.exp(s - n)
        c = jnp.exp(m - n)
        l = c * l + jnp.sum(p, axis=-1, keepdims=True)
        acc = c * acc + jnp.dot(p.astype(BF16), vt, preferred_element_type=F32)
        return n, l, acc

    def two_maps(mask):
        return jnp.concatenate([mask, mask], axis=0)

    meta_mask = two_maps(lax.broadcasted_iota(jnp.int32, (tq, Q_BLOCK), 1) >= PAD_LEN)
    diag_mask = two_maps(lax.broadcasted_iota(jnp.int32, (tq, tq), 1)
                         <= lax.broadcasted_iota(jnp.int32, (tq, tq), 0))

    def q_tile(i, _):
        r0 = pl.multiple_of(Q_BLOCK + i * tq, Q_BLOCK)
        q12 = jnp.concatenate([q1_scr[pl.ds(r0, tq), :], q2_scr[pl.ds(r0, tq), :]], axis=0)
        carry = (jnp.full((2 * tq, 1), NEG, F32), jnp.zeros((2 * tq, 1), F32),
                 jnp.zeros((2 * tq, ATTN_V_DIM), F32))
        carry = step(q12, k_scr[0:Q_BLOCK, :], v_ref[0, 0:Q_BLOCK, :], carry, meta_mask)

        def full(j, c):
            k0 = pl.multiple_of(Q_BLOCK + j * tq, Q_BLOCK)
            return step(q12, k_scr[pl.ds(k0, tq), :], v_ref[0, pl.ds(k0, tq), :], c, None)

        carry = lax.fori_loop(0, i, full, carry)
        carry = step(q12, k_scr[pl.ds(r0, tq), :], v_ref[0, pl.ds(r0, tq), :], carry, diag_mask)
        _, l, acc = carry
        a = acc / l
        o = a[:tq] - lam * a[tq:]
        ms = jnp.mean(o * o, axis=-1, keepdims=True)
        o = o * lax.rsqrt(ms + EPS) * sw_ref[...] * (1.0 - lambda_init)
        o_ref[0, pl.ds(pl.multiple_of(i * tq, tq), tq), :] = o.astype(BF16)
        return 0

    lax.fori_loop(0, seq // tq, q_tile, 0)


def _attention(zmain, lam_qk, cos, sin, subln_w, batch, seq, lambda_init, tq):
    p_len = seq + Q_BLOCK
    z3 = zmain.reshape(batch, p_len, zmain.shape[-1])
    kern = functools.partial(_attn_kernel, seq=seq, tq=tq, lambda_init=lambda_init)
    blk = lambda off: pl.BlockSpec((1, p_len, LANES), lambda b, h, off=off: (b, 0, off + h))
    const = lambda shape: pl.BlockSpec(shape, lambda b, h: (0,) * len(shape))
    return pl.pallas_call(
        kern,
        grid=(batch, ATTN_HEADS),
        in_specs=[
            const((4, ATTN_HEAD_DIM)),
            blk(0), blk(ATTN_HEADS), blk(2 * ATTN_HEADS),
            const((p_len, LANES)), const((p_len, LANES)), const((1, ATTN_V_DIM)),
        ],
        out_specs=pl.BlockSpec((1, seq, ATTN_V_DIM), lambda b, h: (b, 0, h)),
        out_shape=jax.ShapeDtypeStruct((batch, seq, ATTN_HEADS * ATTN_V_DIM), BF16),
        scratch_shapes=[pltpu.VMEM((p_len, LANES), BF16)] * 3,
        compiler_params=_params(("parallel", "parallel")),
        name="diff_attn",
    )(lam_qk, z3, z3, z3, cos, sin, subln_w)


def _mlstm_kernel(ib_ref, fb_ref, q_ref, k_ref, v_ref, og_ref, cwq_ref, cwk_ref, cbq_ref, cbk_ref,
                  gi_ref, gf_ref, nw_ref, o_ref,
                  pad_scr, q_scr, k_scr, b_scr, li_scr, c_scr, *, seq):
    p_len = seq + Q_BLOCK
    nc = p_len // MLSTM_CHUNK
    cl = MLSTM_CHUNK
    nh = MLSTM_HEADS
    dk, dv = MLSTM_QK_DIM, MLSTM_V_DIM

    def conv_silu(x_ref, w_ref, b_ref, h):
        cols = slice(h * dk, (h + 1) * dk)
        pad_scr[0:8, :] = jnp.zeros((8, dk), F32)
        pad_scr[8:8 + p_len, :] = x_ref[0, :, cols].astype(F32)
        y = b_ref[:, cols] + w_ref[3:4, cols] * pad_scr[8:8 + p_len, :]
        for j in range(CONV_WIDTH - 1):
            y = y + w_ref[j:j + 1, cols] * pad_scr[5 + j:5 + j + p_len, :]
        return y * (1.0 / (1.0 + jnp.exp(-y)))

    rows = gi_ref.shape[1]
    pos = (lax.broadcasted_iota(jnp.int32, (rows, cl), 0) * cl
           + lax.broadcasted_iota(jnp.int32, (rows, cl), 1))
    valid = pos >= PAD_LEN
    upper = (lax.broadcasted_iota(jnp.int32, (cl, cl), 0)
             <= lax.broadcasted_iota(jnp.int32, (cl, cl), 1)).astype(F32)
    for h in range(nh):
        q_scr[:, h * dk:(h + 1) * dk] = (conv_silu(q_ref, cwq_ref, cbq_ref, h) * (dk ** -0.5)).astype(BF16)
        k_scr[:, h * dk:(h + 1) * dk] = conv_silu(k_ref, cwk_ref, cbk_ref, h)
        fpre = gf_ref[h] + fb_ref[h]
        log_f = jnp.minimum(fpre, 0.0) - jnp.log(1.0 + jnp.exp(-jnp.abs(fpre)))
        log_f = jnp.where(valid, log_f, 0.0)
        li_scr[h] = jnp.where(valid, gi_ref[h] + ib_ref[h], NEG)
        b_scr[h] = jnp.dot(log_f, upper, preferred_element_type=F32,
                           precision=lax.Precision.HIGHEST)

    c_scr[...] = jnp.zeros_like(c_scr)
    r_io = lax.broadcasted_iota(jnp.int32, (cl, cl), 0)
    c_io = lax.broadcasted_iota(jnp.int32, (cl, cl), 1)
    eye = r_io == c_io
    tril = c_io <= r_io

    def to_col(row):
        return jnp.sum(jnp.where(eye, row, 0.0), axis=1, keepdims=True)

    def head_chunk(h, c, r0, m, n):
        b_r = b_scr[h, pl.ds(c, 1), :]
        li_r = li_scr[h, pl.ds(c, 1), :]
        g = b_r[:, cl - 1:cl]
        a_r = g - b_r + li_r
        b_c = to_col(b_r)
        a_c = to_col(a_r)
        dmat = jnp.where(tril, b_c - b_r + li_r, NEG)
        m_inter = b_c + m
        m_t = jnp.maximum(jnp.max(dmat, axis=1, keepdims=True), m_inter)
        qc = q_scr[pl.ds(r0, cl), h * dk:(h + 1) * dk]
        kc = k_scr[pl.ds(r0, cl), h * dk:(h + 1) * dk]
        vc = v_ref[0, pl.ds(r0, cl), h * dv:(h + 1) * dv]
        s = lax.dot_general(qc, kc.astype(BF16), _NT, preferred_element_type=F32) * jnp.exp(dmat - m_t)
        inter_w = jnp.exp(m_inter - m_t)
        num = (jnp.dot(s.astype(BF16), vc, preferred_element_type=F32)
               + inter_w * jnp.dot(qc, c_scr[h].astype(BF16), preferred_element_type=F32))
        nq = (jnp.sum(s, axis=1, keepdims=True)
              + inter_w * jnp.sum(qc.astype(F32) * n, axis=1, keepdims=True))
        hid = num / jnp.maximum(jnp.abs(nq), jnp.exp(-m_t))
        ms = jnp.mean(hid * hid, axis=-1, keepdims=True)
        hn = hid * lax.rsqrt(ms + EPS) * nw_ref[:, h * dv:(h + 1) * dv]
        og = og_ref[0, pl.ds(r0, cl), h * dv:(h + 1) * dv].astype(F32)
        out = hn * (1.0 / (1.0 + jnp.exp(-og)))
        o0 = pl.multiple_of(jnp.maximum(r0 - Q_BLOCK, 0), cl)
        o_ref[0, pl.ds(o0, cl), h * dv:(h + 1) * dv] = out.astype(BF16)

        m_new = jnp.maximum(g + m, jnp.max(a_r, axis=1, keepdims=True))
        w_c = jnp.exp(a_c - m_new)
        decay = jnp.exp(g + m - m_new)
        kw = kc * w_c
        c_scr[h] = decay * c_scr[h] + lax.dot_general(kw.astype(BF16), vc, _TN, preferred_element_type=F32)
        n_new = decay * n + jnp.sum(kw, axis=0, keepdims=True)
        return m_new, n_new

    def chunk(c, carry):
        r0 = pl.multiple_of(c * cl, cl)
        return tuple(head_chunk(h, c, r0, *carry[h]) for h in range(nh))

    init = tuple((jnp.full((1, 1), NEG, F32), jnp.zeros((1, dk), F32)) for _ in range(nh))
    lax.fori_loop(0, nc, chunk, init)


def _mlstm(zmain, gi_rows, gf_rows, i_b, f_b, conv_w, conv_b, norm_w, batch, seq):
    p_len = seq + Q_BLOCK
    z3 = zmain.reshape(batch, p_len, zmain.shape[-1])
    nh = MLSTM_HEADS
    rows = gi_rows.shape[1]
    qk_w = nh * MLSTM_QK_DIM
    v_w = nh * MLSTM_V_DIM
    q_off = 3 * ATTN_HEADS * LANES // qk_w
    v_off = (q_off + 2) * qk_w // v_w
    smem = pl.BlockSpec(memory_space=pltpu.SMEM)
    return pl.pallas_call(
        functools.partial(_mlstm_kernel, seq=seq),
        grid=(batch,),
        in_specs=[
            smem, smem,
            pl.BlockSpec((1, p_len, qk_w), lambda b: (b, 0, q_off)),
            pl.BlockSpec((1, p_len, qk_w), lambda b: (b, 0, q_off + 1)),
            pl.BlockSpec((1, p_len, v_w), lambda b: (b, 0, v_off)),
            pl.BlockSpec((1, p_len, v_w), lambda b: (b, 0, v_off + 1)),
            pl.BlockSpec((CONV_WIDTH, qk_w), lambda b: (0, 0)),
            pl.BlockSpec((CONV_WIDTH, qk_w), lambda b: (0, 1)),
            pl.BlockSpec((1, qk_w), lambda b: (0, 0)),
            pl.BlockSpec((1, qk_w), lambda b: (0, 1)),
            pl.BlockSpec((nh, rows, MLSTM_CHUNK), lambda b: (b, 0, 0)),
            pl.BlockSpec((nh, rows, MLSTM_CHUNK), lambda b: (b, 0, 0)),
            pl.BlockSpec((1, v_w), lambda b: (0, 0)),
        ],
        out_specs=pl.BlockSpec((1, seq, v_w), lambda b: (b, 0, 0)),
        out_shape=jax.ShapeDtypeStruct((batch, seq, v_w), BF16),
        scratch_shapes=[
            pltpu.VMEM((p_len + 8, MLSTM_QK_DIM), F32),
            pltpu.VMEM((p_len, qk_w), BF16),
            pltpu.VMEM((p_len, qk_w), F32),
            pltpu.VMEM((nh, rows, MLSTM_CHUNK), F32),
            pltpu.VMEM((nh, rows, MLSTM_CHUNK), F32),
            pltpu.VMEM((nh, MLSTM_QK_DIM, MLSTM_V_DIM), F32),
        ],
        compiler_params=_params(("parallel",)),
        name="mlstm",
    )(i_b, f_b, z3, z3, z3, z3, conv_w, conv_w, conv_b, conv_b, gi_rows, gf_rows, norm_w)


def _outproj_kernel(a_ref, m_ref, x_ref, wa_ref, wm_ref, nw_ref, wq_ref, x1_ref, hn_ref, q_ref):
    x1 = (x_ref[...]
          + jnp.dot(a_ref[...], wa_ref[...], preferred_element_type=F32)
          + jnp.dot(m_ref[...], wm_ref[...], preferred_element_type=F32))
    x1_ref[...] = x1
    ms = jnp.mean(x1 * x1, axis=-1, keepdims=True)
    hn = (x1 * lax.rsqrt(ms + EPS) * nw_ref[...]).astype(BF16)
    hn_ref[...] = hn
    q = jnp.dot(hn, wq_ref[...], preferred_element_type=F32).astype(BF16)
    for hp in range(2 * PEER_HEADS):
        q_ref[hp] = q[:, hp * PEER_SUB_DIM:(hp + 1) * PEER_SUB_DIM]


def _outproj(attn, ml, x2, w_attn, w_ml, norm_w, w_q, tm):
    rows, d = x2.shape
    half = attn.shape[1]
    single = pl.Buffered(1)
    return pl.pallas_call(
        _outproj_kernel,
        grid=(rows // tm,),
        in_specs=[
            pl.BlockSpec((tm, half), lambda i: (i, 0)),
            pl.BlockSpec((tm, half), lambda i: (i, 0)),
            pl.BlockSpec((tm, d), lambda i: (i, 0)),
            pl.BlockSpec((half, d), lambda i: (0, 0), pipeline_mode=single),
            pl.BlockSpec((half, d), lambda i: (0, 0), pipeline_mode=single),
            pl.BlockSpec((1, d), lambda i: (0, 0)),
            pl.BlockSpec((d, d), lambda i: (0, 0), pipeline_mode=single),
        ],
        out_specs=[
            pl.BlockSpec((tm, d), lambda i: (i, 0)),
            pl.BlockSpec((tm, d), lambda i: (i, 0)),
            pl.BlockSpec((2 * PEER_HEADS, tm, PEER_SUB_DIM), lambda i: (0, i, 0)),
        ],
        out_shape=[
            jax.ShapeDtypeStruct((rows, d), F32),
            jax.ShapeDtypeStruct((rows, d), BF16),
            jax.ShapeDtypeStruct((2 * PEER_HEADS, rows, PEER_SUB_DIM), BF16),
        ],
        compiler_params=_params(("parallel",)),
        name="outproj",
    )(attn, ml, x2, w_attn, w_ml, norm_w, w_q)


ROUTE_TOK = LANES
W_ROW_STRIDE = PEER_N_KEYS + 8


def _topk_rows(s, pos_rows, payload):
    sentinel = jnp.int32(2 ** 30)
    k_io = lax.broadcasted_iota(jnp.int32, (PEER_TOPK, s.shape[1]), 0)
    vals = jnp.zeros((PEER_TOPK, s.shape[1]), F32)
    poss = jnp.zeros((PEER_TOPK, s.shape[1]), jnp.int32)
    pays = jnp.zeros((PEER_TOPK, s.shape[1]), jnp.int32)
    for it in range(PEER_TOPK):
        m = jnp.max(s, axis=0, keepdims=True)
        pos = jnp.min(jnp.where(s == m, pos_rows, sentinel), axis=0, keepdims=True)
        hit = pos_rows == pos
        vals = jnp.where(k_io == it, m, vals)
        poss = jnp.where(k_io == it, pos, poss)
        if payload is not None:
            pay = jnp.max(jnp.where(hit, payload, -1), axis=0, keepdims=True)
            pays = jnp.where(k_io == it, pay, pays)
        s = jnp.where(hit, -jnp.inf, s)
    return vals, poss, pays


def _sorting_network(n):
    pairs, p = [], 1
    while p < n:
        k = p
        while k >= 1:
            for j in range(k % p, n - k, 2 * k):
                for i in range(min(k, n - j - k)):
                    if (i + j) // (2 * p) == (i + j + k) // (2 * p):
                        pairs.append((i + j, i + j + k))
            k //= 2
        p *= 2
    return pairs


def _topk_columns(s):
    n_lvl = s.shape[0] // SUBLANES
    vs = [s[l * SUBLANES:(l + 1) * SUBLANES] for l in range(n_lvl)]
    col = lax.broadcasted_iota(jnp.int32, vs[0].shape, 0)
    ix = [col + l * SUBLANES for l in range(n_lvl)]
    for a, b in _sorting_network(n_lvl):
        va, vb, ia, ib = vs[a], vs[b], ix[a], ix[b]
        gt, lt = va > vb, va < vb
        vs[a], vs[b] = jnp.maximum(va, vb), jnp.minimum(va, vb)
        ix[a] = jnp.where(gt, ia, jnp.where(lt, ib, jnp.minimum(ia, ib)))
        ix[b] = jnp.where(gt, ib, jnp.where(lt, ia, jnp.maximum(ia, ib)))
    sentinel = jnp.int32(2 ** 30)
    k_io = lax.broadcasted_iota(jnp.int32, (PEER_TOPK, s.shape[1]), 0)
    vals = jnp.zeros((PEER_TOPK, s.shape[1]), F32)
    rows = jnp.zeros((PEER_TOPK, s.shape[1]), jnp.int32)
    for it in range(PEER_TOPK):
        m = jnp.max(vs[0], axis=0, keepdims=True)
        r = jnp.min(jnp.where(vs[0] == m, ix[0], sentinel), axis=0, keepdims=True)
        vals = jnp.where(k_io == it, m, vals)
        rows = jnp.where(k_io == it, r, rows)
        hit = ix[0] == r
        for l in range(PEER_TOPK - 1 - it):
            vs[l] = jnp.where(hit, vs[l + 1], vs[l])
            ix[l] = jnp.where(hit, ix[l + 1], ix[l])
    return vals, rows


def _pair_candidates(a, b, combine):
    rows = [combine(a[0:1], b)]
    rows += [combine(a[i:i + 1], b[0:SUBLANES]) for i in range(1, SUBLANES)]
    rows += [combine(a[SUBLANES:], b[0:1])]
    return jnp.concatenate(rows, axis=0)


def _pair_positions(n_tok):
    k = PEER_TOPK
    r = lax.broadcasted_iota(jnp.int32, (k + (SUBLANES - 1) * SUBLANES + SUBLANES, n_tok), 0)
    mid = (1 + lax.shift_right_logical(r - k, 3)) * k + lax.bitwise_and(r - k, SUBLANES - 1)
    tail = (r - (k + (SUBLANES - 1) * SUBLANES) + SUBLANES) * k
    return jnp.where(r < k, r, jnp.where(r < k + (SUBLANES - 1) * SUBLANES, mid, tail))


def _route_kernel(q_ref, keys_ref, w_ref, g_scr, i1_scr, i2_scr, gt_scr, i1t_scr, i2t_scr, wt_scr):
    nk = PEER_N_KEYS
    pair_pos = _pair_positions(ROUTE_TOK)

    def stage1(h):
        s1 = lax.dot_general(keys_ref[2 * h], q_ref[2 * h], _NT, preferred_element_type=F32)
        s2 = lax.dot_general(keys_ref[2 * h + 1], q_ref[2 * h + 1], _NT, preferred_element_type=F32)
        return _topk_columns(s1) + _topk_columns(s2)

    def stage2(h, tops):
        v1, p1, v2, p2 = tops
        cand_s = _pair_candidates(v1, v2, lambda a, b: a + b)
        cand_i = _pair_candidates(p1, p2, lambda a, b: a * nk + b)
        best, _, idx = _topk_rows(cand_s, pair_pos, cand_i)
        e = jnp.exp(best - best[0:1])
        gate = e / jnp.sum(e, axis=0, keepdims=True)
        r0 = pl.multiple_of(h * PEER_TOPK, PEER_TOPK)
        g_scr[pl.ds(r0, PEER_TOPK), :] = gate
        i1_scr[pl.ds(r0, PEER_TOPK), :] = lax.shift_right_logical(idx, 7).astype(F32)
        i2_scr[pl.ds(r0, PEER_TOPK), :] = lax.bitwise_and(idx, nk - 1).astype(F32)

    def head(h, tops):
        nxt = stage1(h + 1)
        stage2(h, tops)
        return nxt

    stage2(PEER_HEADS - 1, lax.fori_loop(0, PEER_HEADS - 1, head, stage1(0)))

    gt_scr[...] = g_scr[...].T
    i1t_scr[...] = i1_scr[...].T
    i2t_scr[...] = i2_scr[...].T

    sub = lax.broadcasted_iota(jnp.int32, (nk, nk), 0).astype(F32)

    def token(t, _):
        g_row = gt_scr[pl.ds(t, 1), :]
        lhs = jnp.where(i1t_scr[pl.ds(t, 1), :] == sub, g_row, 0.0).astype(BF16)
        rhs = jnp.where(i2t_scr[pl.ds(t, 1), :] == sub, 1.0, 0.0).astype(BF16)
        w_t = lax.dot_general(lhs, rhs, _NT, preferred_element_type=F32)
        wt_scr[pl.ds(pl.multiple_of(t * W_ROW_STRIDE, W_ROW_STRIDE), nk), :] = w_t
        return 0

    lax.fori_loop(0, ROUTE_TOK, token, 0, unroll=16)

    def key_row(a, _):
        w_ref[a] = wt_scr[pl.ds(a, ROUTE_TOK, stride=W_ROW_STRIDE), :].astype(BF16)
        return 0

    lax.fori_loop(0, nk, key_row, 0, unroll=4)


def _route(q3, keys):
    nhp, rows, sd = q3.shape
    nk = PEER_N_KEYS
    return pl.pallas_call(
        _route_kernel,
        grid=(rows // ROUTE_TOK,),
        in_specs=[
            pl.BlockSpec((nhp, ROUTE_TOK, sd), lambda i: (0, i, 0)),
            pl.BlockSpec((nhp, nk, sd), lambda i: (0, 0, 0)),
        ],
        out_specs=pl.BlockSpec((nk, ROUTE_TOK, nk), lambda i: (0, i, 0)),
        out_shape=jax.ShapeDtypeStruct((nk, rows, nk), BF16),
        scratch_shapes=[pltpu.VMEM((PEER_HEADS * PEER_TOPK, ROUTE_TOK), F32)] * 3
        + [pltpu.VMEM((ROUTE_TOK, PEER_HEADS * PEER_TOPK), F32)] * 3
        + [pltpu.VMEM((ROUTE_TOK * W_ROW_STRIDE, nk), F32)],
        compiler_params=_params(("parallel",)),
        name="peer_route",
    )(q3, keys)


def _expert_kernel(hn_ref, u_ref, v_ref, w_ref, x1_ref, nw_ref, o_ref):
    j = pl.program_id(1)

    @pl.when(j == 0)
    def _():
        o_ref[...] = x1_ref[...]

    act = lax.dot_general(hn_ref[...], u_ref[...], _NT, preferred_element_type=F32)
    gel = 0.5 * act * (1.0 + lax.erf(act * (2.0 ** -0.5)))
    wts = jnp.concatenate([w_ref[a] for a in range(w_ref.shape[0])], axis=1)
    coef = (gel * wts.astype(F32)).astype(BF16)
    o_ref[...] += jnp.dot(coef, v_ref[...], preferred_element_type=F32)

    @pl.when(j == pl.num_programs(1) - 1)
    def _():
        y = o_ref[...]
        ms = jnp.mean(y * y, axis=-1, keepdims=True)
        o_ref[...] = y * lax.rsqrt(ms + EPS) * nw_ref[...]


def _experts(hn, u, v, w3, x1, norm_w, tb, te):
    rows, d = hn.shape
    ne = u.shape[0]
    ka = te // PEER_N_KEYS
    return pl.pallas_call(
        _expert_kernel,
        grid=(rows // tb, ne // te),
        in_specs=[
            pl.BlockSpec((tb, d), lambda i, j: (i, 0), pipeline_mode=pl.Buffered(1)),
            pl.BlockSpec((te, d), lambda i, j: (j, 0)),
            pl.BlockSpec((te, d), lambda i, j: (j, 0)),
            pl.BlockSpec((ka, tb, PEER_N_KEYS), lambda i, j: (j, i, 0)),
            pl.BlockSpec((tb, d), lambda i, j: (i, 0), pipeline_mode=pl.Buffered(1)),
            pl.BlockSpec((1, d), lambda i, j: (0, 0)),
        ],
        out_specs=pl.BlockSpec((tb, d), lambda i, j: (i, 0)),
        out_shape=jax.ShapeDtypeStruct((rows, d), F32),
        compiler_params=_params(("parallel", "arbitrary")),
        name="peer_experts",
    )(hn, u, v, w3, x1, norm_w)


def _rope_tables(p_len):
    pos = jnp.maximum(jnp.arange(p_len) - PAD_LEN, 0).astype(F32)
    inv_freq = ROPE_THETA ** (-jnp.arange(0, ATTN_HEAD_DIM, 2, dtype=F32) / ATTN_HEAD_DIM)
    ang = pos[:, None] * inv_freq[None, :]
    cos = jnp.tile(jnp.cos(ang), (1, 4))
    sin = jnp.sin(ang)
    return cos, jnp.tile(jnp.concatenate([-sin, sin], axis=-1), (1, 2))


def _pick(n, candidates):
    for c in candidates:
        if n % c == 0:
            return c
    raise ValueError(f"no tile in {candidates} divides {n}")


def kernel(x, meta_tokens, norm_mix_w, w_in, attn_lambda_qk, attn_subln_w, mlstm_conv_w, mlstm_conv_b, mlstm_i_b, mlstm_f_b, mlstm_norm_w, w_out, norm_ffn_w, peer_w_q, peer_sub_keys, peer_u, peer_v, norm_final_w):
    batch, seq, d = x.shape
    assert w_in.shape[0] == 1, "single-layer block"
    p_len = seq + Q_BLOCK
    lambda_init = 0.8 - 0.6 * math.exp(-0.3 * 0)
    n_main = w_in.shape[-1] - 2 * MLSTM_HEADS

    head_rows = jnp.concatenate([jnp.zeros((PAD_LEN, d), x.dtype), meta_tokens.astype(x.dtype)], axis=0)
    hp = jnp.concatenate([jnp.broadcast_to(head_rows[None], (batch, Q_BLOCK, d)), x], axis=1)
    hp = hp.reshape(batch * p_len, d)

    w_main = w_in[0, :, :n_main].astype(BF16)
    w_gate = jnp.pad(w_in[0, :, n_main:], ((0, 0), (0, LANES - 2 * MLSTM_HEADS))).astype(BF16)
    zmain, zgate = _inproj(hp, norm_mix_w, w_main, w_gate,
                           tm=_pick(batch * p_len, (1024, 512, 256, 128)), tn=_pick(n_main, (1024, 512)))

    cos, sin = _rope_tables(p_len)
    attn = _attention(zmain, attn_lambda_qk[0], cos, sin, attn_subln_w, batch, seq, lambda_init,
                      tq=_pick(seq, (512, 256, 128)))

    nc = p_len // MLSTM_CHUNK
    rows = -(-nc // 8) * 8
    gates = zgate[:, :2 * MLSTM_HEADS].reshape(batch, nc, MLSTM_CHUNK, 2, MLSTM_HEADS)
    gates = jnp.pad(gates.transpose(3, 0, 4, 1, 2), ((0, 0), (0, 0), (0, 0), (0, rows - nc), (0, 0)))
    gates = gates.reshape(2, batch * MLSTM_HEADS, rows, MLSTM_CHUNK)
    ml = _mlstm(zmain, gates[0], gates[1], mlstm_i_b[0], mlstm_f_b[0], mlstm_conv_w[0], mlstm_conv_b,
                mlstm_norm_w, batch, seq)

    half = ATTN_HEADS * ATTN_V_DIM
    x1, hn, q3 = _outproj(attn.reshape(batch * seq, half), ml.reshape(batch * seq, d - half),
                          x.reshape(batch * seq, d), w_out[0, :half].astype(BF16), w_out[0, half:].astype(BF16),
                          norm_ffn_w, peer_w_q[0].astype(BF16), tm=_pick(batch * seq, (512, 256, 128)))

    keys = peer_sub_keys[0].reshape(2 * PEER_HEADS, PEER_N_KEYS, PEER_SUB_DIM).astype(BF16)
    w3 = _route(q3, keys)

    out = _experts(hn, peer_u[0].astype(BF16), peer_v[0].astype(BF16), w3, x1, norm_final_w[None],
                   tb=_pick(batch * seq, (1024, 512, 256, 128)), te=1024)
    return out.reshape(batch, seq, d)
```

```python
import functools
import math

import jax
import jax.numpy as jnp
from jax import lax
from jax.experimental import pallas as pl
from jax.experimental.pallas import tpu as pltpu

F32 = jnp.float32
BF16 = jnp.bfloat16

N_META = 16
Q_BLOCK = 128
PAD_LEN = Q_BLOCK - N_META
ATTN_HEADS = 8
ATTN_HEAD_DIM = 64
ATTN_V_DIM = 128
ROPE_THETA = 10000.0
MLSTM_HEADS = 4
MLSTM_QK_DIM = 128
MLSTM_V_DIM = 256
MLSTM_CHUNK = 64
CONV_WIDTH = 4
PEER_HEADS = 8
PEER_N_KEYS = 128
PEER_SUB_DIM = 128
PEER_TOPK = 16
EPS = 1e-6
NEG = -1e30

LANES = 128
SUBLANES = 8
VMEM_LIMIT = 56 * 1024 * 1024

_NT = (((1,), (1,)), ((), ()))
_TN = (((0,), (0,)), ((), ()))


def _params(sem):
    return pltpu.CompilerParams(dimension_semantics=sem, vmem_limit_bytes=VMEM_LIMIT)


def _inproj_kernel(x_ref, nw_ref, w_ref, wg_ref, z_ref, zg_ref, h_scr):
    @pl.when(pl.program_id(1) == 0)
    def _():
        x = x_ref[...]
        ms = jnp.mean(x * x, axis=-1, keepdims=True)
        h_scr[...] = (x * lax.rsqrt(ms + EPS) * nw_ref[...]).astype(BF16)
        zg_ref[...] = jnp.dot(h_scr[...], wg_ref[...], preferred_element_type=F32)

    z_ref[...] = jnp.dot(h_scr[...], w_ref[...], preferred_element_type=F32).astype(BF16)


def _inproj(hp, norm_w, w_main, w_gate, tm, tn):
    rows, d = hp.shape
    cols = w_main.shape[1]
    return pl.pallas_call(
        _inproj_kernel,
        grid=(rows // tm, cols // tn),
        in_specs=[
            pl.BlockSpec((tm, d), lambda i, j: (i, 0)),
            pl.BlockSpec((1, d), lambda i, j: (0, 0)),
            pl.BlockSpec((d, tn), lambda i, j: (0, j)),
            pl.BlockSpec((d, LANES), lambda i, j: (0, 0)),
        ],
        out_specs=[
            pl.BlockSpec((tm, tn), lambda i, j: (i, j)),
            pl.BlockSpec((tm, LANES), lambda i, j: (i, 0)),
        ],
        out_shape=[
            jax.ShapeDtypeStruct((rows, cols), BF16),
            jax.ShapeDtypeStruct((rows, LANES), F32),
        ],
        scratch_shapes=[pltpu.VMEM((tm, d), BF16)],
        compiler_params=_params(("parallel", "arbitrary")),
        name="inproj",
    )(hp, norm_w, w_main, w_gate)


def _attn_kernel(lq_ref, q_ref, k_ref, v_ref, cos_ref, sin_ref, sw_ref, o_ref,
                 q1t_scr, q2t_scr, k_scr, vtm_scr, vt_scr, *, seq, tq, cw, lambda_init):
    p_len = seq + Q_BLOCK
    n_tiles = seq // tq
    lane = lax.broadcasted_iota(jnp.int32, (p_len, LANES), 1)
    low_half = (lane % ATTN_HEAD_DIM) < (ATTN_HEAD_DIM // 2)
    first_map = lax.broadcasted_iota(jnp.int32, (seq, LANES), 1) < ATTN_HEAD_DIM

    def rope(x):
        rot = jnp.where(low_half, pltpu.roll(x, LANES - 32, 1), pltpu.roll(x, 32, 1))
        return x * cos_ref[...] + rot * sin_ref[...]

    q = (rope(q_ref[0].astype(F32)) * (ATTN_HEAD_DIM ** -0.5))[Q_BLOCK:]
    k_scr[...] = rope(k_ref[0].astype(F32)).astype(BF16)
    q1 = jnp.where(first_map, q, 0.0)
    q2 = jnp.where(first_map, 0.0, q)
    vtm_scr[...] = v_ref[0, 0:Q_BLOCK, :].astype(F32).T.astype(BF16)
    for j in range(n_tiles):
        rows = slice(j * tq, (j + 1) * tq)
        q1t_scr[:, rows] = q1[rows].T.astype(BF16)
        q2t_scr[:, rows] = q2[rows].T.astype(BF16)
        vt_scr[j] = v_ref[0, Q_BLOCK + j * tq:Q_BLOCK + (j + 1) * tq, :].astype(F32).T.astype(BF16)

    lq = lq_ref[...]
    lam = (jnp.exp(jnp.sum(lq[0:1] * lq[1:2], axis=-1, keepdims=True))
           - jnp.exp(jnp.sum(lq[2:3] * lq[3:4], axis=-1, keepdims=True)) + lambda_init)

    n_chain = 2 * tq // cw

    def step(qts, kt, vtt, carry, masks):
        out = []
        for qt, (m, l, acc), mask in zip(qts, carry, masks):
            s = jnp.dot(kt, qt, preferred_element_type=F32)
            if mask is not None:
                s = jnp.where(mask, s, NEG)
            n = jnp.maximum(m, jnp.max(s, axis=0, keepdims=True))
            p = jnp.exp(s - n)
            c = jnp.exp(m - n)
            l = c * l + jnp.sum(p, axis=0, keepdims=True)
            acc = c * acc + jnp.dot(vtt, p.astype(BF16), preferred_element_type=F32)
            out.append((n, l, acc))
        return tuple(out)

    no_masks = (None,) * n_chain
    meta_masks = (lax.broadcasted_iota(jnp.int32, (Q_BLOCK, cw), 0) >= PAD_LEN,) * n_chain
    key_io = lax.broadcasted_iota(jnp.int32, (tq, cw), 0)
    qry_io = lax.broadcasted_iota(jnp.int32, (tq, cw), 1)
    diag_masks = tuple(key_io <= lax.bitwise_and(qry_io + c * cw, tq - 1) for c in range(n_chain))

    for i in range(n_tiles):
        cols = slice(i * tq, (i + 1) * tq)
        q12t = jnp.concatenate([q1t_scr[:, cols], q2t_scr[:, cols]], axis=1)
        qts = tuple(q12t[:, c * cw:(c + 1) * cw] for c in range(n_chain))
        carry = tuple((jnp.full((1, cw), NEG, F32), jnp.zeros((1, cw), F32), jnp.zeros((ATTN_V_DIM, cw), F32))
                      for _ in range(n_chain))
        carry = step(qts, k_scr[0:Q_BLOCK, :], vtm_scr[...], carry, meta_masks)

        def full(j, c, qts=qts):
            k0 = pl.multiple_of(Q_BLOCK + j * tq, Q_BLOCK)
            return step(qts, k_scr[pl.ds(k0, tq), :], vt_scr[j], c, no_masks)

        carry = lax.fori_loop(0, i, full, carry)
        carry = step(qts, k_scr[Q_BLOCK + i * tq:Q_BLOCK + (i + 1) * tq, :], vt_scr[i], carry, diag_masks)
        a = jnp.concatenate([acc / l for _, l, acc in carry], axis=1)
        o = (a[:, :tq] - lam * a[:, tq:]).T
        ms = jnp.mean(o * o, axis=-1, keepdims=True)
        o = o * lax.rsqrt(ms + EPS) * sw_ref[...] * (1.0 - lambda_init)
        o_ref[0, cols, :] = o.astype(BF16)


def _attention(zmain, lam_qk, cos, sin, subln_w, batch, seq, lambda_init, tq):
    assert tq & (tq - 1) == 0, "query tile must be a power of two"
    p_len = seq + Q_BLOCK
    z3 = zmain.reshape(batch, p_len, zmain.shape[-1])
    kern = functools.partial(_attn_kernel, seq=seq, tq=tq, cw=2 * tq, lambda_init=lambda_init)
    blk = lambda off: pl.BlockSpec((1, p_len, LANES), lambda b, h, off=off: (b, 0, off + h))
    const = lambda shape: pl.BlockSpec(shape, lambda b, h: (0,) * len(shape))
    return pl.pallas_call(
        kern,
        grid=(batch, ATTN_HEADS),
        in_specs=[
            const((4, ATTN_HEAD_DIM)),
            blk(0), blk(ATTN_HEADS), blk(2 * ATTN_HEADS),
            const((p_len, LANES)), const((p_len, LANES)), const((1, ATTN_V_DIM)),
        ],
        out_specs=pl.BlockSpec((1, seq, ATTN_V_DIM), lambda b, h: (b, 0, h)),
        out_shape=jax.ShapeDtypeStruct((batch, seq, ATTN_HEADS * ATTN_V_DIM), BF16),
        scratch_shapes=[
            pltpu.VMEM((LANES, seq), BF16),
            pltpu.VMEM((LANES, seq), BF16),
            pltpu.VMEM((p_len, LANES), BF16),
            pltpu.VMEM((ATTN_V_DIM, Q_BLOCK), BF16),
            pltpu.VMEM((seq // tq, ATTN_V_DIM, tq), BF16),
        ],
        compiler_params=_params(("parallel", "parallel")),
        name="diff_attn",
    )(lam_qk, z3, z3, z3, cos, sin, subln_w)


def _mlstm_kernel(ib_ref, fb_ref, q_ref, k_ref, v_ref, og_ref, cwq_ref, cwk_ref, cbq_ref, cbk_ref,
                  gi_ref, gf_ref, nw_ref, o_ref,
                  pad_scr, q_scr, k_scr, b_scr, li_scr, c_scr, *, seq):
    p_len = seq + Q_BLOCK
    nc = p_len // MLSTM_CHUNK
    cl = MLSTM_CHUNK
    nh = MLSTM_HEADS
    dk, dv = MLSTM_QK_DIM, MLSTM_V_DIM

    def conv_silu(x_ref, w_ref, b_ref, h):
        cols = slice(h * dk, (h + 1) * dk)
        pad_scr[0:8, :] = jnp.zeros((8, dk), F32)
        pad_scr[8:8 + p_len, :] = x_ref[0, :, cols].astype(F32)
        y = b_ref[:, cols] + w_ref[3:4, cols] * pad_scr[8:8 + p_len, :]
        for j in range(CONV_WIDTH - 1):
            y = y + w_ref[j:j + 1, cols] * pad_scr[5 + j:5 + j + p_len, :]
        return y * (1.0 / (1.0 + jnp.exp(-y)))

    rows = gi_ref.shape[1]
    pos = (lax.broadcasted_iota(jnp.int32, (rows, cl), 0) * cl
           + lax.broadcasted_iota(jnp.int32, (rows, cl), 1))
    valid = pos >= PAD_LEN
    upper = (lax.broadcasted_iota(jnp.int32, (cl, cl), 0)
             <= lax.broadcasted_iota(jnp.int32, (cl, cl), 1)).astype(F32)
    for h in range(nh):
        q_scr[:, h * dk:(h + 1) * dk] = (conv_silu(q_ref, cwq_ref, cbq_ref, h) * (dk ** -0.5)).astype(BF16)
        k_scr[:, h * dk:(h + 1) * dk] = conv_silu(k_ref, cwk_ref, cbk_ref, h)
        fpre = gf_ref[h] + fb_ref[h]
        log_f = jnp.minimum(fpre, 0.0) - jnp.log(1.0 + jnp.exp(-jnp.abs(fpre)))
        log_f = jnp.where(valid, log_f, 0.0)
        li_scr[h] = jnp.where(valid, gi_ref[h] + ib_ref[h], NEG)
        b_scr[h] = jnp.dot(log_f, upper, preferred_element_type=F32,
                           precision=lax.Precision.HIGHEST)

    c_scr[...] = jnp.zeros_like(c_scr)
    r_io = lax.broadcasted_iota(jnp.int32, (cl, cl), 0)
    c_io = lax.broadcasted_iota(jnp.int32, (cl, cl), 1)
    eye = r_io == c_io
    tril = c_io <= r_io

    def to_col(row):
        return jnp.sum(jnp.where(eye, row, 0.0), axis=1, keepdims=True)

    def head_chunk(h, c, r0, m, n):
        b_r = b_scr[h, pl.ds(c, 1), :]
        li_r = li_scr[h, pl.ds(c, 1), :]
        g = b_r[:, cl - 1:cl]
        a_r = g - b_r + li_r
        b_c = to_col(b_r)
        a_c = to_col(a_r)
        dmat = jnp.where(tril, b_c - b_r + li_r, NEG)
        m_inter = b_c + m
        m_t = jnp.maximum(jnp.max(dmat, axis=1, keepdims=True), m_inter)
        qc = q_scr[pl.ds(r0, cl), h * dk:(h + 1) * dk]
        kc = k_scr[pl.ds(r0, cl), h * dk:(h + 1) * dk]
        vc = v_ref[0, pl.ds(r0, cl), h * dv:(h + 1) * dv]
        s = lax.dot_general(qc, kc.astype(BF16), _NT, preferred_element_type=F32) * jnp.exp(dmat - m_t)
        inter_w = jnp.exp(m_inter - m_t)
        num = (jnp.dot(s.astype(BF16), vc, preferred_element_type=F32)
               + inter_w * jnp.dot(qc, c_scr[h].astype(BF16), preferred_element_type=F32))
        nq = (jnp.sum(s, axis=1, keepdims=True)
              + inter_w * jnp.sum(qc.astype(F32) * n, axis=1, keepdims=True))
        hid = num / jnp.maximum(jnp.abs(nq), jnp.exp(-m_t))
        ms = jnp.mean(hid * hid, axis=-1, keepdims=True)
        hn = hid * lax.rsqrt(ms + EPS) * nw_ref[:, h * dv:(h + 1) * dv]
        og = og_ref[0, pl.ds(r0, cl), h * dv:(h + 1) * dv].astype(F32)
        out = hn * (1.0 / (1.0 + jnp.exp(-og)))
        o0 = pl.multiple_of(jnp.maximum(r0 - Q_BLOCK, 0), cl)
        o_ref[0, pl.ds(o0, cl), h * dv:(h + 1) * dv] = out.astype(BF16)

        m_new = jnp.maximum(g + m, jnp.max(a_r, axis=1, keepdims=True))
        w_c = jnp.exp(a_c - m_new)
        decay = jnp.exp(g + m - m_new)
        kw = kc * w_c
        c_scr[h] = decay * c_scr[h] + lax.dot_general(kw.astype(BF16), vc, _TN, preferred_element_type=F32)
        n_new = decay * n + jnp.sum(kw, axis=0, keepdims=True)
        return m_new, n_new

    def chunk(c, carry):
        r0 = pl.multiple_of(c * cl, cl)
        return tuple(head_chunk(h, c, r0, *carry[h]) for h in range(nh))

    init = tuple((jnp.full((1, 1), NEG, F32), jnp.zeros((1, dk), F32)) for _ in range(nh))
    lax.fori_loop(0, nc, chunk, init, unroll=2 if nc % 2 == 0 else 1)


def _mlstm(zmain, gi_rows, gf_rows, i_b, f_b, conv_w, conv_b, norm_w, batch, seq):
    p_len = seq + Q_BLOCK
    z3 = zmain.reshape(batch, p_len, zmain.shape[-1])
    nh = MLSTM_HEADS
    rows = gi_rows.shape[1]
    qk_w = nh * MLSTM_QK_DIM
    v_w = nh * MLSTM_V_DIM
    q_off = 3 * ATTN_HEADS * LANES // qk_w
    v_off = (q_off + 2) * qk_w // v_w
    smem = pl.BlockSpec(memory_space=pltpu.SMEM)
    return pl.pallas_call(
        functools.partial(_mlstm_kernel, seq=seq),
        grid=(batch,),
        in_specs=[
            smem, smem,
            pl.BlockSpec((1, p_len, qk_w), lambda b: (b, 0, q_off)),
            pl.BlockSpec((1, p_len, qk_w), lambda b: (b, 0, q_off + 1)),
            pl.BlockSpec((1, p_len, v_w), lambda b: (b, 0, v_off)),
            pl.BlockSpec((1, p_len, v_w), lambda b: (b, 0, v_off + 1)),
            pl.BlockSpec((CONV_WIDTH, qk_w), lambda b: (0, 0)),
            pl.BlockSpec((CONV_WIDTH, qk_w), lambda b: (0, 1)),
            pl.BlockSpec((1, qk_w), lambda b: (0, 0)),
            pl.BlockSpec((1, qk_w), lambda b: (0, 1)),
            pl.BlockSpec((nh, rows, MLSTM_CHUNK), lambda b: (b, 0, 0)),
            pl.BlockSpec((nh, rows, MLSTM_CHUNK), lambda b: (b, 0, 0)),
            pl.BlockSpec((1, v_w), lambda b: (0, 0)),
        ],
        out_specs=pl.BlockSpec((1, seq, v_w), lambda b: (b, 0, 0)),
        out_shape=jax.ShapeDtypeStruct((batch, seq, v_w), BF16),
        scratch_shapes=[
            pltpu.VMEM((p_len + 8, MLSTM_QK_DIM), F32),
            pltpu.VMEM((p_len, qk_w), BF16),
            pltpu.VMEM((p_len, qk_w), F32),
            pltpu.VMEM((nh, rows, MLSTM_CHUNK), F32),
            pltpu.VMEM((nh, rows, MLSTM_CHUNK), F32),
            pltpu.VMEM((nh, MLSTM_QK_DIM, MLSTM_V_DIM), F32),
        ],
        compiler_params=_params(("parallel",)),
        name="mlstm",
    )(i_b, f_b, z3, z3, z3, z3, conv_w, conv_w, conv_b, conv_b, gi_rows, gf_rows, norm_w)


def _outproj_kernel(a_ref, m_ref, x_ref, wa_ref, wm_ref, nw_ref, wq_ref, x1_ref, hn_ref, q_ref):
    x1 = (x_ref[...]
          + jnp.dot(a_ref[...], wa_ref[...], preferred_element_type=F32)
          + jnp.dot(m_ref[...], wm_ref[...], preferred_element_type=F32))
    x1_ref[...] = x1
    ms = jnp.mean(x1 * x1, axis=-1, keepdims=True)
    hn = (x1 * lax.rsqrt(ms + EPS) * nw_ref[...]).astype(BF16)
    hn_ref[...] = hn
    q = jnp.dot(hn, wq_ref[...], preferred_element_type=F32).astype(BF16)
    for hp in range(2 * PEER_HEADS):
        q_ref[hp] = q[:, hp * PEER_SUB_DIM:(hp + 1) * PEER_SUB_DIM]


def _outproj(attn, ml, x2, w_attn, w_ml, norm_w, w_q, tm):
    rows, d = x2.shape
    half = attn.shape[1]
    single = pl.Buffered(1)
    return pl.pallas_call(
        _outproj_kernel,
        grid=(rows // tm,),
        in_specs=[
            pl.BlockSpec((tm, half), lambda i: (i, 0)),
            pl.BlockSpec((tm, half), lambda i: (i, 0)),
            pl.BlockSpec((tm, d), lambda i: (i, 0)),
            pl.BlockSpec((half, d), lambda i: (0, 0), pipeline_mode=single),
            pl.BlockSpec((half, d), lambda i: (0, 0), pipeline_mode=single),
            pl.BlockSpec((1, d), lambda i: (0, 0)),
            pl.BlockSpec((d, d), lambda i: (0, 0), pipeline_mode=single),
        ],
        out_specs=[
            pl.BlockSpec((tm, d), lambda i: (i, 0)),
            pl.BlockSpec((tm, d), lambda i: (i, 0)),
            pl.BlockSpec((2 * PEER_HEADS, tm, PEER_SUB_DIM), lambda i: (0, i, 0)),
        ],
        out_shape=[
            jax.ShapeDtypeStruct((rows, d), F32),
            jax.ShapeDtypeStruct((rows, d), BF16),
            jax.ShapeDtypeStruct((2 * PEER_HEADS, rows, PEER_SUB_DIM), BF16),
        ],
        compiler_params=_params(("parallel",)),
        name="outproj",
    )(attn, ml, x2, w_attn, w_ml, norm_w, w_q)


ROUTE_TOK = LANES
W_ROW_STRIDE = PEER_N_KEYS + 8


def _topk_rows(s, pos_rows, payload):
    sentinel = jnp.int32(2 ** 30)
    k_io = lax.broadcasted_iota(jnp.int32, (PEER_TOPK, s.shape[1]), 0)
    vals = jnp.zeros((PEER_TOPK, s.shape[1]), F32)
    poss = jnp.zeros((PEER_TOPK, s.shape[1]), jnp.int32)
    pays = jnp.zeros((PEER_TOPK, s.shape[1]), jnp.int32)
    for it in range(PEER_TOPK):
        m = jnp.max(s, axis=0, keepdims=True)
        pos = jnp.min(jnp.where(s == m, pos_rows, sentinel), axis=0, keepdims=True)
        hit = pos_rows == pos
        vals = jnp.where(k_io == it, m, vals)
        poss = jnp.where(k_io == it, pos, poss)
        if payload is not None:
            pay = jnp.max(jnp.where(hit, payload, -1), axis=0, keepdims=True)
            pays = jnp.where(k_io == it, pay, pays)
        s = jnp.where(hit, -jnp.inf, s)
    return vals, poss, pays


def _sorting_network(n):
    pairs, p = [], 1
    while p < n:
        k = p
        while k >= 1:
            for j in range(k % p, n - k, 2 * k):
                for i in range(min(k, n - j - k)):
                    if (i + j) // (2 * p) == (i + j + k) // (2 * p):
                        pairs.append((i + j, i + j + k))
            k //= 2
        p *= 2
    return pairs


def _topk_columns(s):
    n_lvl = s.shape[0] // SUBLANES
    vs = [s[l * SUBLANES:(l + 1) * SUBLANES] for l in range(n_lvl)]
    col = lax.broadcasted_iota(jnp.int32, vs[0].shape, 0)
    ix = [col + l * SUBLANES for l in range(n_lvl)]
    for a, b in _sorting_network(n_lvl):
        va, vb, ia, ib = vs[a], vs[b], ix[a], ix[b]
        gt, lt = va > vb, va < vb
        vs[a], vs[b] = jnp.maximum(va, vb), jnp.minimum(va, vb)
        ix[a] = jnp.where(gt, ia, jnp.where(lt, ib, jnp.minimum(ia, ib)))
        ix[b] = jnp.where(gt, ib, jnp.where(lt, ia, jnp.maximum(ia, ib)))
    sentinel = jnp.int32(2 ** 30)
    k_io = lax.broadcasted_iota(jnp.int32, (PEER_TOPK, s.shape[1]), 0)
    vals = jnp.zeros((PEER_TOPK, s.shape[1]), F32)
    rows = jnp.zeros((PEER_TOPK, s.shape[1]), jnp.int32)
    for it in range(PEER_TOPK):
        m = jnp.max(vs[0], axis=0, keepdims=True)
        r = jnp.min(jnp.where(vs[0] == m, ix[0], sentinel), axis=0, keepdims=True)
        vals = jnp.where(k_io == it, m, vals)
        rows = jnp.where(k_io == it, r, rows)
        hit = ix[0] == r
        for l in range(PEER_TOPK - 1 - it):
            vs[l] = jnp.where(hit, vs[l + 1], vs[l])
            ix[l] = jnp.where(hit, ix[l + 1], ix[l])
    return vals, rows


def _pair_candidates(a, b, combine):
    rows = [combine(a[0:1], b)]
    rows += [combine(a[i:i + 1], b[0:SUBLANES]) for i in range(1, SUBLANES)]
    rows += [combine(a[SUBLANES:], b[0:1])]
    return jnp.concatenate(rows, axis=0)


def _pair_positions(n_tok):
    k = PEER_TOPK
    r = lax.broadcasted_iota(jnp.int32, (k + (SUBLANES - 1) * SUBLANES + SUBLANES, n_tok), 0)
    mid = (1 + lax.shift_right_logical(r - k, 3)) * k + lax.bitwise_and(r - k, SUBLANES - 1)
    tail = (r - (k + (SUBLANES - 1) * SUBLANES) + SUBLANES) * k
    return jnp.where(r < k, r, jnp.where(r < k + (SUBLANES - 1) * SUBLANES, mid, tail))


def _route_kernel(q_ref, keys_ref, w_ref, g_scr, i1_scr, i2_scr, gt_scr, i1t_scr, i2t_scr, wt_scr):
    nk = PEER_N_KEYS
    pair_pos = _pair_positions(ROUTE_TOK)

    def stage1(h):
        s1 = lax.dot_general(keys_ref[2 * h], q_ref[2 * h], _NT, preferred_element_type=F32)
        s2 = lax.dot_general(keys_ref[2 * h + 1], q_ref[2 * h + 1], _NT, preferred_element_type=F32)
        return _topk_columns(s1) + _topk_columns(s2)

    def stage2(h, tops):
        v1, p1, v2, p2 = tops
        cand_s = _pair_candidates(v1, v2, lambda a, b: a + b)
        cand_i = _pair_candidates(p1, p2, lambda a, b: a * nk + b)
        best, _, idx = _topk_rows(cand_s, pair_pos, cand_i)
        e = jnp.exp(best - best[0:1])
        gate = e / jnp.sum(e, axis=0, keepdims=True)
        r0 = pl.multiple_of(h * PEER_TOPK, PEER_TOPK)
        g_scr[pl.ds(r0, PEER_TOPK), :] = gate
        i1_scr[pl.ds(r0, PEER_TOPK), :] = lax.shift_right_logical(idx, 7).astype(F32)
        i2_scr[pl.ds(r0, PEER_TOPK), :] = lax.bitwise_and(idx, nk - 1).astype(F32)

    sub = lax.broadcasted_iota(jnp.int32, (nk, nk), 0).astype(F32)
    tok_per_head = ROUTE_TOK // PEER_HEADS

    def build_w(blk):
        for u in range(tok_per_head):
            t = blk * tok_per_head + u
            g_row = gt_scr[pl.ds(t, 1), :]
            lhs = jnp.where(i1t_scr[pl.ds(t, 1), :] == sub, g_row, 0.0).astype(BF16)
            rhs = jnp.where(i2t_scr[pl.ds(t, 1), :] == sub, 1.0, 0.0).astype(BF16)
            w_t = lax.dot_general(lhs, rhs, _NT, preferred_element_type=F32)
            wt_scr[pl.ds(pl.multiple_of(t * W_ROW_STRIDE, SUBLANES), nk), :] = w_t

    @pl.when(pl.program_id(0) == 0)
    def _():
        gt_scr[...] = jnp.zeros_like(gt_scr)
        i1t_scr[...] = jnp.zeros_like(i1t_scr)
        i2t_scr[...] = jnp.zeros_like(i2t_scr)

    def head(h, tops):
        nxt = stage1(h + 1)
        stage2(h, tops)
        build_w(h)
        return nxt

    stage2(PEER_HEADS - 1, lax.fori_loop(0, PEER_HEADS - 1, head, stage1(0)))
    build_w(PEER_HEADS - 1)

    def key_row(a, _):
        w_ref[a] = wt_scr[pl.ds(a, ROUTE_TOK, stride=W_ROW_STRIDE), :].astype(BF16)
        return 0

    lax.fori_loop(0, nk, key_row, 0, unroll=4)

    gt_scr[...] = g_scr[...].T
    i1t_scr[...] = i1_scr[...].T
    i2t_scr[...] = i2_scr[...].T


def _route(q3, keys):
    nhp, rows, sd = q3.shape
    nk = PEER_N_KEYS
    nblk = rows // ROUTE_TOK
    return pl.pallas_call(
        _route_kernel,
        grid=(nblk + 1,),
        in_specs=[
            pl.BlockSpec((nhp, ROUTE_TOK, sd), lambda i: (0, jnp.minimum(i, nblk - 1), 0)),
            pl.BlockSpec((nhp, nk, sd), lambda i: (0, 0, 0)),
        ],
        out_specs=pl.BlockSpec((nk, ROUTE_TOK, nk), lambda i: (0, jnp.maximum(i - 1, 0), 0)),
        out_shape=jax.ShapeDtypeStruct((nk, rows, nk), BF16),
        scratch_shapes=[pltpu.VMEM((PEER_HEADS * PEER_TOPK, ROUTE_TOK), F32)] * 3
        + [pltpu.VMEM((ROUTE_TOK, PEER_HEADS * PEER_TOPK), F32)] * 3
        + [pltpu.VMEM((ROUTE_TOK * W_ROW_STRIDE, nk), F32)],
        compiler_params=_params(("arbitrary",)),
        name="peer_route",
    )(q3, keys)


def _expert_kernel(hn_ref, u_ref, v_ref, w_ref, x1_ref, nw_ref, o_ref):
    j = pl.program_id(1)

    @pl.when(j == 0)
    def _():
        o_ref[...] = x1_ref[...]

    act = lax.dot_general(hn_ref[...], u_ref[...], _NT, preferred_element_type=F32)
    gel = 0.5 * act * (1.0 + lax.erf(act * (2.0 ** -0.5)))
    wts = jnp.concatenate([w_ref[a] for a in range(w_ref.shape[0])], axis=1)
    coef = (gel * wts.astype(F32)).astype(BF16)
    o_ref[...] += jnp.dot(coef, v_ref[...], preferred_element_type=F32)

    @pl.when(j == pl.num_programs(1) - 1)
    def _():
        y = o_ref[...]
        ms = jnp.mean(y * y, axis=-1, keepdims=True)
        o_ref[...] = y * lax.rsqrt(ms + EPS) * nw_ref[...]


def _experts(hn, u, v, w3, x1, norm_w, tb, te):
    rows, d = hn.shape
    ne = u.shape[0]
    ka = te // PEER_N_KEYS
    return pl.pallas_call(
        _expert_kernel,
        grid=(rows // tb, ne // te),
        in_specs=[
            pl.BlockSpec((tb, d), lambda i, j: (i, 0), pipeline_mode=pl.Buffered(1)),
            pl.BlockSpec((te, d), lambda i, j: (j, 0)),
            pl.BlockSpec((te, d), lambda i, j: (j, 0)),
            pl.BlockSpec((ka, tb, PEER_N_KEYS), lambda i, j: (j, i, 0)),
            pl.BlockSpec((tb, d), lambda i, j: (i, 0), pipeline_mode=pl.Buffered(1)),
            pl.BlockSpec((1, d), lambda i, j: (0, 0)),
        ],
        out_specs=pl.BlockSpec((tb, d), lambda i, j: (i, 0)),
        out_shape=jax.ShapeDtypeStruct((rows, d), F32),
        compiler_params=_params(("parallel", "arbitrary")),
        name="peer_experts",
    )(hn, u, v, w3, x1, norm_w)


def _rope_tables(p_len):
    pos = jnp.maximum(jnp.arange(p_len) - PAD_LEN, 0).astype(F32)
    inv_freq = ROPE_THETA ** (-jnp.arange(0, ATTN_HEAD_DIM, 2, dtype=F32) / ATTN_HEAD_DIM)
    ang = pos[:, None] * inv_freq[None, :]
    cos = jnp.tile(jnp.cos(ang), (1, 4))
    sin = jnp.sin(ang)
    return cos, jnp.tile(jnp.concatenate([-sin, sin], axis=-1), (1, 2))


def _pick(n, candidates):
    for c in candidates:
        if n % c == 0:
            return c
    raise ValueError(f"no tile in {candidates} divides {n}")


def kernel(x, meta_tokens, norm_mix_w, w_in, attn_lambda_qk, attn_subln_w, mlstm_conv_w, mlstm_conv_b, mlstm_i_b, mlstm_f_b, mlstm_norm_w, w_out, norm_ffn_w, peer_w_q, peer_sub_keys, peer_u, peer_v, norm_final_w):
    batch, seq, d = x.shape
    assert w_in.shape[0] == 1, "single-layer block"
    p_len = seq + Q_BLOCK
    lambda_init = 0.8 - 0.6 * math.exp(-0.3 * 0)
    n_main = w_in.shape[-1] - 2 * MLSTM_HEADS

    head_rows = jnp.concatenate([jnp.zeros((PAD_LEN, d), x.dtype), meta_tokens.astype(x.dtype)], axis=0)
    hp = jnp.concatenate([jnp.broadcast_to(head_rows[None], (batch, Q_BLOCK, d)), x], axis=1)
    hp = hp.reshape(batch * p_len, d)

    w_main = w_in[0, :, :n_main].astype(BF16)
    w_gate = jnp.pad(w_in[0, :, n_main:], ((0, 0), (0, LANES - 2 * MLSTM_HEADS))).astype(BF16)
    zmain, zgate = _inproj(hp, norm_mix_w, w_main, w_gate,
                           tm=_pick(batch * p_len, (1024, 512, 256, 128)), tn=_pick(n_main, (1024, 512)))

    cos, sin = _rope_tables(p_len)
    attn = _attention(zmain, attn_lambda_qk[0], cos, sin, attn_subln_w, batch, seq, lambda_init,
                      tq=_pick(seq, (512, 256, 128)))

    nc = p_len // MLSTM_CHUNK
    rows = -(-nc // 8) * 8
    gates = zgate[:, :2 * MLSTM_HEADS].reshape(batch, nc, MLSTM_CHUNK, 2, MLSTM_HEADS)
    gates = jnp.pad(gates.transpose(3, 0, 4, 1, 2), ((0, 0), (0, 0), (0, 0), (0, rows - nc), (0, 0)))
    gates = gates.reshape(2, batch * MLSTM_HEADS, rows, MLSTM_CHUNK)
    ml = _mlstm(zmain, gates[0], gates[1], mlstm_i_b[0], mlstm_f_b[0], mlstm_conv_w[0], mlstm_conv_b,
                mlstm_norm_w, batch, seq)

    half = ATTN_HEADS * ATTN_V_DIM
    x1, hn, q3 = _outproj(attn.reshape(batch * seq, half), ml.reshape(batch * seq, d - half),
                          x.reshape(batch * seq, d), w_out[0, :half].astype(BF16), w_out[0, half:].astype(BF16),
                          norm_ffn_w, peer_w_q[0].astype(BF16), tm=_pick(batch * seq, (512, 256, 128)))

    keys = peer_sub_keys[0].reshape(2 * PEER_HEADS, PEER_N_KEYS, PEER_SUB_DIM).astype(BF16)
    w3 = _route(q3, keys)

    out = _experts(hn, peer_u[0].astype(BF16), peer_v[0].astype(BF16), w3, x1, norm_final_w[None],
                   tb=_pick(batch * seq, (1024, 512, 256, 128)), te=1024)
    return out.reshape(batch, seq, d)
```

```python
import functools
import math

import jax
import jax.numpy as jnp
from jax import lax
from jax.experimental import pallas as pl
from jax.experimental.pallas import tpu as pltpu

F32 = jnp.float32
BF16 = jnp.bfloat16

N_META = 16
Q_BLOCK = 128
PAD_LEN = Q_BLOCK - N_META
ATTN_HEADS = 8
ATTN_HEAD_DIM = 64
ATTN_V_DIM = 128
ROPE_THETA = 10000.0
MLSTM_HEADS = 4
MLSTM_QK_DIM = 128
MLSTM_V_DIM = 256
MLSTM_CHUNK = 64
CONV_WIDTH = 4
PEER_HEADS = 8
PEER_N_KEYS = 128
PEER_SUB_DIM = 128
PEER_TOPK = 16
EPS = 1e-6
NEG = -1e30

LANES = 128
SUBLANES = 8
VMEM_LIMIT = 56 * 1024 * 1024

_NT = (((1,), (1,)), ((), ()))
_TN = (((0,), (0,)), ((), ()))


def _params(sem):
    return pltpu.CompilerParams(dimension_semantics=sem, vmem_limit_bytes=VMEM_LIMIT)


def _inproj_kernel(x_ref, nw_ref, w_ref, wg_ref, z_ref, zg_ref, h_scr):
    @pl.when(pl.program_id(1) == 0)
    def _():
        x = x_ref[...]
        ms = jnp.mean(x * x, axis=-1, keepdims=True)
        h_scr[...] = (x * lax.rsqrt(ms + EPS) * nw_ref[...]).astype(BF16)
        zg_ref[...] = jnp.dot(h_scr[...], wg_ref[...], preferred_element_type=F32)

    z_ref[...] = jnp.dot(h_scr[...], w_ref[...], preferred_element_type=F32).astype(BF16)


def _inproj(hp, norm_w, w_main, w_gate, tm, tn):
    rows, d = hp.shape
    cols = w_main.shape[1]
    return pl.pallas_call(
        _inproj_kernel,
        grid=(rows // tm, cols // tn),
        in_specs=[
            pl.BlockSpec((tm, d), lambda i, j: (i, 0)),
            pl.BlockSpec((1, d), lambda i, j: (0, 0)),
            pl.BlockSpec((d, tn), lambda i, j: (0, j)),
            pl.BlockSpec((d, LANES), lambda i, j: (0, 0)),
        ],
        out_specs=[
            pl.BlockSpec((tm, tn), lambda i, j: (i, j)),
            pl.BlockSpec((tm, LANES), lambda i, j: (i, 0)),
        ],
        out_shape=[
            jax.ShapeDtypeStruct((rows, cols), BF16),
            jax.ShapeDtypeStruct((rows, LANES), F32),
        ],
        scratch_shapes=[pltpu.VMEM((tm, d), BF16)],
        compiler_params=_params(("parallel", "arbitrary")),
        name="inproj",
    )(hp, norm_w, w_main, w_gate)


def _attn_kernel(lq_ref, q_ref, k_ref, v_ref, km_ref, vm_ref, cos_ref, sin_ref, sw_ref, o_ref,
                 q1t_scr, q2t_scr, k_scr, vtm_scr, vt_scr, *, seq, tq, cw, lambda_init):
    n_tiles = seq // tq

    def rope(x, r0):
        n = x.shape[0]
        low_half = (lax.broadcasted_iota(jnp.int32, x.shape, 1) % ATTN_HEAD_DIM) < (ATTN_HEAD_DIM // 2)
        rot = jnp.where(low_half, pltpu.roll(x, LANES - 32, 1), pltpu.roll(x, 32, 1))
        return x * cos_ref[r0:r0 + n, :] + rot * sin_ref[r0:r0 + n, :]

    first_map = lax.broadcasted_iota(jnp.int32, (seq, LANES), 1) < ATTN_HEAD_DIM
    q = rope(q_ref[0].astype(F32), 0) * (ATTN_HEAD_DIM ** -0.5)
    k_scr[0:seq, :] = rope(k_ref[0].astype(F32), 0).astype(BF16)
    k_scr[seq:seq + Q_BLOCK, :] = rope(km_ref[...].astype(F32), seq).astype(BF16)
    q1 = jnp.where(first_map, q, 0.0)
    q2 = jnp.where(first_map, 0.0, q)
    vtm_scr[...] = vm_ref[...].astype(F32).T.astype(BF16)
    for j in range(n_tiles):
        rows = slice(j * tq, (j + 1) * tq)
        q1t_scr[:, rows] = q1[rows].T.astype(BF16)
        q2t_scr[:, rows] = q2[rows].T.astype(BF16)
        vt_scr[j] = v_ref[0, rows, :].astype(F32).T.astype(BF16)

    lq = lq_ref[...]
    lam = (jnp.exp(jnp.sum(lq[0:1] * lq[1:2], axis=-1, keepdims=True))
           - jnp.exp(jnp.sum(lq[2:3] * lq[3:4], axis=-1, keepdims=True)) + lambda_init)

    n_chain = 2 * tq // cw

    def step(qts, kt, vtt, carry, masks):
        out = []
        for qt, (m, l, acc), mask in zip(qts, carry, masks):
            s = jnp.dot(kt, qt, preferred_element_type=F32)
            if mask is not None:
                s = jnp.where(mask, s, NEG)
            n = jnp.maximum(m, jnp.max(s, axis=0, keepdims=True))
            p = jnp.exp(s - n)
            c = jnp.exp(m - n)
            l = c * l + jnp.sum(p, axis=0, keepdims=True)
            acc = c * acc + jnp.dot(vtt, p.astype(BF16), preferred_element_type=F32)
            out.append((n, l, acc))
        return tuple(out)

    no_masks = (None,) * n_chain
    meta_masks = (lax.broadcasted_iota(jnp.int32, (Q_BLOCK, cw), 0) >= PAD_LEN,) * n_chain
    key_io = lax.broadcasted_iota(jnp.int32, (tq, cw), 0)
    qry_io = lax.broadcasted_iota(jnp.int32, (tq, cw), 1)
    diag_masks = tuple(key_io <= lax.bitwise_and(qry_io + c * cw, tq - 1) for c in range(n_chain))

    for i in range(n_tiles):
        cols = slice(i * tq, (i + 1) * tq)
        q12t = jnp.concatenate([q1t_scr[:, cols], q2t_scr[:, cols]], axis=1)
        qts = tuple(q12t[:, c * cw:(c + 1) * cw] for c in range(n_chain))
        carry = tuple((jnp.full((1, cw), NEG, F32), jnp.zeros((1, cw), F32), jnp.zeros((ATTN_V_DIM, cw), F32))
                      for _ in range(n_chain))
        carry = step(qts, k_scr[seq:seq + Q_BLOCK, :], vtm_scr[...], carry, meta_masks)

        def full(j, c, qts=qts):
            k0 = pl.multiple_of(j * tq, tq)
            return step(qts, k_scr[pl.ds(k0, tq), :], vt_scr[j], c, no_masks)

        carry = lax.fori_loop(0, i, full, carry)
        carry = step(qts, k_scr[cols, :], vt_scr[i], carry, diag_masks)
        a = jnp.concatenate([acc / l for _, l, acc in carry], axis=1)
        o = (a[:, :tq] - lam * a[:, tq:]).T
        ms = jnp.mean(o * o, axis=-1, keepdims=True)
        o = o * lax.rsqrt(ms + EPS) * sw_ref[...] * (1.0 - lambda_init)
        o_ref[0, cols, :] = o.astype(BF16)


def _attention(ztok, zmeta, lam_qk, cos, sin, subln_w, batch, seq, lambda_init, tq):
    assert tq & (tq - 1) == 0, "query tile must be a power of two"
    p_len = seq + Q_BLOCK
    z3 = ztok.reshape(batch, seq, ztok.shape[-1])
    kern = functools.partial(_attn_kernel, seq=seq, tq=tq, cw=2 * tq, lambda_init=lambda_init)
    blk = lambda off: pl.BlockSpec((1, seq, LANES), lambda b, h, off=off: (b, 0, off + h))
    mblk = lambda off: pl.BlockSpec((Q_BLOCK, LANES), lambda b, h, off=off: (0, off + h))
    const = lambda shape: pl.BlockSpec(shape, lambda b, h: (0,) * len(shape))
    return pl.pallas_call(
        kern,
        grid=(batch, ATTN_HEADS),
        in_specs=[
            const((4, ATTN_HEAD_DIM)),
            blk(0), blk(ATTN_HEADS), blk(2 * ATTN_HEADS),
            mblk(ATTN_HEADS), mblk(2 * ATTN_HEADS),
            const((p_len, LANES)), const((p_len, LANES)), const((1, ATTN_V_DIM)),
        ],
        out_specs=pl.BlockSpec((1, seq, ATTN_V_DIM), lambda b, h: (b, 0, h)),
        out_shape=jax.ShapeDtypeStruct((batch, seq, ATTN_HEADS * ATTN_V_DIM), BF16),
        scratch_shapes=[
            pltpu.VMEM((LANES, seq), BF16),
            pltpu.VMEM((LANES, seq), BF16),
            pltpu.VMEM((p_len, LANES), BF16),
            pltpu.VMEM((ATTN_V_DIM, Q_BLOCK), BF16),
            pltpu.VMEM((seq // tq, ATTN_V_DIM, tq), BF16),
        ],
        compiler_params=_params(("parallel", "parallel")),
        name="diff_attn",
    )(lam_qk, z3, z3, z3, zmeta, zmeta, cos, sin, subln_w)


def _mlstm_kernel(ib_ref, fb_ref, q_ref, k_ref, v_ref, og_ref, qm_ref, km_ref, vm_ref,
                  cwq_ref, cwk_ref, cbq_ref, cbk_ref, gi_ref, gf_ref, nw_ref, o_ref,
                  pad_scr, q_scr, k_scr, b_scr, li_scr, c_scr, *, seq):
    p_len = seq + Q_BLOCK
    nc = p_len // MLSTM_CHUNK
    cl = MLSTM_CHUNK
    nh = MLSTM_HEADS
    dk, dv = MLSTM_QK_DIM, MLSTM_V_DIM

    def conv_silu(x_ref, xm_ref, w_ref, b_ref, h):
        cols = slice(h * dk, (h + 1) * dk)
        pad_scr[0:8, :] = jnp.zeros((8, dk), F32)
        pad_scr[8:8 + Q_BLOCK, :] = xm_ref[:, cols].astype(F32)
        pad_scr[8 + Q_BLOCK:8 + p_len, :] = x_ref[0, :, cols].astype(F32)
        y = b_ref[:, cols] + w_ref[3:4, cols] * pad_scr[8:8 + p_len, :]
        for j in range(CONV_WIDTH - 1):
            y = y + w_ref[j:j + 1, cols] * pad_scr[5 + j:5 + j + p_len, :]
        return y * (1.0 / (1.0 + jnp.exp(-y)))

    rows = gi_ref.shape[1]
    pos = (lax.broadcasted_iota(jnp.int32, (rows, cl), 0) * cl
           + lax.broadcasted_iota(jnp.int32, (rows, cl), 1))
    valid = pos >= PAD_LEN
    upper = (lax.broadcasted_iota(jnp.int32, (cl, cl), 0)
             <= lax.broadcasted_iota(jnp.int32, (cl, cl), 1)).astype(F32)
    for h in range(nh):
        q_scr[:, h * dk:(h + 1) * dk] = (conv_silu(q_ref, qm_ref, cwq_ref, cbq_ref, h) * (dk ** -0.5)).astype(BF16)
        k_scr[:, h * dk:(h + 1) * dk] = conv_silu(k_ref, km_ref, cwk_ref, cbk_ref, h)
        fpre = gf_ref[h] + fb_ref[h]
        log_f = jnp.minimum(fpre, 0.0) - jnp.log(1.0 + jnp.exp(-jnp.abs(fpre)))
        log_f = jnp.where(valid, log_f, 0.0)
        li_scr[h] = jnp.where(valid, gi_ref[h] + ib_ref[h], NEG)
        b_scr[h] = jnp.dot(log_f, upper, preferred_element_type=F32,
                           precision=lax.Precision.HIGHEST)

    c_scr[...] = jnp.zeros_like(c_scr)
    r_io = lax.broadcasted_iota(jnp.int32, (cl, cl), 0)
    c_io = lax.broadcasted_iota(jnp.int32, (cl, cl), 1)
    eye = r_io == c_io
    tril = c_io <= r_io

    def to_col(row):
        return jnp.sum(jnp.where(eye, row, 0.0), axis=1, keepdims=True)

    def head_chunk(h, c, r0, m, n, in_meta):
        b_r = b_scr[h, pl.ds(c, 1), :]
        li_r = li_scr[h, pl.ds(c, 1), :]
        g = b_r[:, cl - 1:cl]
        a_r = g - b_r + li_r
        b_c = to_col(b_r)
        a_c = to_col(a_r)
        dmat = jnp.where(tril, b_c - b_r + li_r, NEG)
        m_inter = b_c + m
        m_t = jnp.maximum(jnp.max(dmat, axis=1, keepdims=True), m_inter)
        qc = q_scr[pl.ds(r0, cl), h * dk:(h + 1) * dk]
        kc = k_scr[pl.ds(r0, cl), h * dk:(h + 1) * dk]
        if in_meta:
            vc = vm_ref[r0:r0 + cl, h * dv:(h + 1) * dv]
        else:
            t0 = pl.multiple_of(r0 - Q_BLOCK, cl)
            vc = v_ref[0, pl.ds(t0, cl), h * dv:(h + 1) * dv]
        s = lax.dot_general(qc, kc.astype(BF16), _NT, preferred_element_type=F32) * jnp.exp(dmat - m_t)
        inter_w = jnp.exp(m_inter - m_t)
        num = (jnp.dot(s.astype(BF16), vc, preferred_element_type=F32)
               + inter_w * jnp.dot(qc, c_scr[h].astype(BF16), preferred_element_type=F32))
        nq = (jnp.sum(s, axis=1, keepdims=True)
              + inter_w * jnp.sum(qc.astype(F32) * n, axis=1, keepdims=True))
        if not in_meta:
            hid = num / jnp.maximum(jnp.abs(nq), jnp.exp(-m_t))
            ms = jnp.mean(hid * hid, axis=-1, keepdims=True)
            hn = hid * lax.rsqrt(ms + EPS) * nw_ref[:, h * dv:(h + 1) * dv]
            og = og_ref[0, pl.ds(t0, cl), h * dv:(h + 1) * dv].astype(F32)
            out = hn * (1.0 / (1.0 + jnp.exp(-og)))
            o_ref[0, pl.ds(t0, cl), h * dv:(h + 1) * dv] = out.astype(BF16)

        m_new = jnp.maximum(g + m, jnp.max(a_r, axis=1, keepdims=True))
        w_c = jnp.exp(a_c - m_new)
        decay = jnp.exp(g + m - m_new)
        kw = kc * w_c
        c_scr[h] = decay * c_scr[h] + lax.dot_general(kw.astype(BF16), vc, _TN, preferred_element_type=F32)
        n_new = decay * n + jnp.sum(kw, axis=0, keepdims=True)
        return m_new, n_new

    def chunk(c, carry):
        r0 = pl.multiple_of(c * cl, cl)
        return tuple(head_chunk(h, c, r0, *carry[h], in_meta=False) for h in range(nh))

    carry = tuple((jnp.full((1, 1), NEG, F32), jnp.zeros((1, dk), F32)) for _ in range(nh))
    n_meta = Q_BLOCK // cl
    for c in range(n_meta):
        carry = tuple(head_chunk(h, c, c * cl, *carry[h], in_meta=True) for h in range(nh))
    lax.fori_loop(n_meta, nc, chunk, carry, unroll=2 if (nc - n_meta) % 2 == 0 else 1)


def _mlstm(ztok, zmeta, gi_rows, gf_rows, i_b, f_b, conv_w, conv_b, norm_w, batch, seq):
    p_len = seq + Q_BLOCK
    z3 = ztok.reshape(batch, seq, ztok.shape[-1])
    nh = MLSTM_HEADS
    rows = gi_rows.shape[1]
    qk_w = nh * MLSTM_QK_DIM
    v_w = nh * MLSTM_V_DIM
    q_off = 3 * ATTN_HEADS * LANES // qk_w
    v_off = (q_off + 2) * qk_w // v_w
    smem = pl.BlockSpec(memory_space=pltpu.SMEM)
    return pl.pallas_call(
        functools.partial(_mlstm_kernel, seq=seq),
        grid=(batch,),
        in_specs=[
            smem, smem,
            pl.BlockSpec((1, seq, qk_w), lambda b: (b, 0, q_off)),
            pl.BlockSpec((1, seq, qk_w), lambda b: (b, 0, q_off + 1)),
            pl.BlockSpec((1, seq, v_w), lambda b: (b, 0, v_off)),
            pl.BlockSpec((1, seq, v_w), lambda b: (b, 0, v_off + 1)),
            pl.BlockSpec((Q_BLOCK, qk_w), lambda b: (0, q_off)),
            pl.BlockSpec((Q_BLOCK, qk_w), lambda b: (0, q_off + 1)),
            pl.BlockSpec((Q_BLOCK, v_w), lambda b: (0, v_off)),
            pl.BlockSpec((CONV_WIDTH, qk_w), lambda b: (0, 0)),
            pl.BlockSpec((CONV_WIDTH, qk_w), lambda b: (0, 1)),
            pl.BlockSpec((1, qk_w), lambda b: (0, 0)),
            pl.BlockSpec((1, qk_w), lambda b: (0, 1)),
            pl.BlockSpec((nh, rows, MLSTM_CHUNK), lambda b: (b, 0, 0)),
            pl.BlockSpec((nh, rows, MLSTM_CHUNK), lambda b: (b, 0, 0)),
            pl.BlockSpec((1, v_w), lambda b: (0, 0)),
        ],
        out_specs=pl.BlockSpec((1, seq, v_w), lambda b: (b, 0, 0)),
        out_shape=jax.ShapeDtypeStruct((batch, seq, v_w), BF16),
        scratch_shapes=[
            pltpu.VMEM((p_len + 8, MLSTM_QK_DIM), F32),
            pltpu.VMEM((p_len, qk_w), BF16),
            pltpu.VMEM((p_len, qk_w), F32),
            pltpu.VMEM((nh, rows, MLSTM_CHUNK), F32),
            pltpu.VMEM((nh, rows, MLSTM_CHUNK), F32),
            pltpu.VMEM((nh, MLSTM_QK_DIM, MLSTM_V_DIM), F32),
        ],
        compiler_params=_params(("parallel",)),
        name="mlstm",
    )(i_b, f_b, z3, z3, z3, z3, zmeta, zmeta, zmeta, conv_w, conv_w, conv_b, conv_b, gi_rows, gf_rows, norm_w)


def _outproj_kernel(a_ref, m_ref, x_ref, wa_ref, wm_ref, nw_ref, wq_ref, x1_ref, hn_ref, q_ref):
    x1 = (x_ref[...]
          + jnp.dot(a_ref[...], wa_ref[...], preferred_element_type=F32)
          + jnp.dot(m_ref[...], wm_ref[...], preferred_element_type=F32))
    x1_ref[...] = x1
    ms = jnp.mean(x1 * x1, axis=-1, keepdims=True)
    hn = (x1 * lax.rsqrt(ms + EPS) * nw_ref[...]).astype(BF16)
    hn_ref[...] = hn
    q = jnp.dot(hn, wq_ref[...], preferred_element_type=F32).astype(BF16)
    for hp in range(2 * PEER_HEADS):
        q_ref[hp] = q[:, hp * PEER_SUB_DIM:(hp + 1) * PEER_SUB_DIM]


def _outproj(attn, ml, x2, w_attn, w_ml, norm_w, w_q, tm):
    rows, d = x2.shape
    half = attn.shape[1]
    single = pl.Buffered(1)
    return pl.pallas_call(
        _outproj_kernel,
        grid=(rows // tm,),
        in_specs=[
            pl.BlockSpec((tm, half), lambda i: (i, 0)),
            pl.BlockSpec((tm, half), lambda i: (i, 0)),
            pl.BlockSpec((tm, d), lambda i: (i, 0)),
            pl.BlockSpec((half, d), lambda i: (0, 0), pipeline_mode=single),
            pl.BlockSpec((half, d), lambda i: (0, 0), pipeline_mode=single),
            pl.BlockSpec((1, d), lambda i: (0, 0)),
            pl.BlockSpec((d, d), lambda i: (0, 0), pipeline_mode=single),
        ],
        out_specs=[
            pl.BlockSpec((tm, d), lambda i: (i, 0)),
            pl.BlockSpec((tm, d), lambda i: (i, 0)),
            pl.BlockSpec((2 * PEER_HEADS, tm, PEER_SUB_DIM), lambda i: (0, i, 0)),
        ],
        out_shape=[
            jax.ShapeDtypeStruct((rows, d), F32),
            jax.ShapeDtypeStruct((rows, d), BF16),
            jax.ShapeDtypeStruct((2 * PEER_HEADS, rows, PEER_SUB_DIM), BF16),
        ],
        compiler_params=_params(("parallel",)),
        name="outproj",
    )(attn, ml, x2, w_attn, w_ml, norm_w, w_q)


ROUTE_TOK = LANES
W_ROW_STRIDE = PEER_N_KEYS + 8


def _topk_rows(s, pos_rows, payload):
    sentinel = jnp.int32(2 ** 30)
    k_io = lax.broadcasted_iota(jnp.int32, (PEER_TOPK, s.shape[1]), 0)
    vals = jnp.zeros((PEER_TOPK, s.shape[1]), F32)
    poss = jnp.zeros((PEER_TOPK, s.shape[1]), jnp.int32)
    pays = jnp.zeros((PEER_TOPK, s.shape[1]), jnp.int32)
    for it in range(PEER_TOPK):
        m = jnp.max(s, axis=0, keepdims=True)
        pos = jnp.min(jnp.where(s == m, pos_rows, sentinel), axis=0, keepdims=True)
        hit = pos_rows == pos
        vals = jnp.where(k_io == it, m, vals)
        poss = jnp.where(k_io == it, pos, poss)
        if payload is not None:
            pay = jnp.max(jnp.where(hit, payload, -1), axis=0, keepdims=True)
            pays = jnp.where(k_io == it, pay, pays)
        s = jnp.where(hit, -jnp.inf, s)
    return vals, poss, pays


def _sorting_network(n):
    pairs, p = [], 1
    while p < n:
        k = p
        while k >= 1:
            for j in range(k % p, n - k, 2 * k):
                for i in range(min(k, n - j - k)):
                    if (i + j) // (2 * p) == (i + j + k) // (2 * p):
                        pairs.append((i + j, i + j + k))
            k //= 2
        p *= 2
    return pairs


def _topk_columns(s):
    n_lvl = s.shape[0] // SUBLANES
    vs = [s[l * SUBLANES:(l + 1) * SUBLANES] for l in range(n_lvl)]
    col = lax.broadcasted_iota(jnp.int32, vs[0].shape, 0)
    ix = [col + l * SUBLANES for l in range(n_lvl)]
    for a, b in _sorting_network(n_lvl):
        va, vb, ia, ib = vs[a], vs[b], ix[a], ix[b]
        gt, lt = va > vb, va < vb
        vs[a], vs[b] = jnp.maximum(va, vb), jnp.minimum(va, vb)
        ix[a] = jnp.where(gt, ia, jnp.where(lt, ib, jnp.minimum(ia, ib)))
        ix[b] = jnp.where(gt, ib, jnp.where(lt, ia, jnp.maximum(ia, ib)))
    sentinel = jnp.int32(2 ** 30)
    k_io = lax.broadcasted_iota(jnp.int32, (PEER_TOPK, s.shape[1]), 0)
    vals = jnp.zeros((PEER_TOPK, s.shape[1]), F32)
    rows = jnp.zeros((PEER_TOPK, s.shape[1]), jnp.int32)
    for it in range(PEER_TOPK):
        m = jnp.max(vs[0], axis=0, keepdims=True)
        r = jnp.min(jnp.where(vs[0] == m, ix[0], sentinel), axis=0, keepdims=True)
        vals = jnp.where(k_io == it, m, vals)
        rows = jnp.where(k_io == it, r, rows)
        hit = ix[0] == r
        for l in range(PEER_TOPK - 1 - it):
            vs[l] = jnp.where(hit, vs[l + 1], vs[l])
            ix[l] = jnp.where(hit, ix[l + 1], ix[l])
    return vals, rows


def _pair_candidates(a, b, combine):
    rows = [combine(a[0:1], b)]
    rows += [combine(a[i:i + 1], b[0:SUBLANES]) for i in range(1, SUBLANES)]
    rows += [combine(a[SUBLANES:], b[0:1])]
    return jnp.concatenate(rows, axis=0)


def _pair_positions(n_tok):
    k = PEER_TOPK
    r = lax.broadcasted_iota(jnp.int32, (k + (SUBLANES - 1) * SUBLANES + SUBLANES, n_tok), 0)
    mid = (1 + lax.shift_right_logical(r - k, 3)) * k + lax.bitwise_and(r - k, SUBLANES - 1)
    tail = (r - (k + (SUBLANES - 1) * SUBLANES) + SUBLANES) * k
    return jnp.where(r < k, r, jnp.where(r < k + (SUBLANES - 1) * SUBLANES, mid, tail))


def _route_kernel(q_ref, keys_ref, w_ref, g_scr, i1_scr, i2_scr, gt_scr, i1t_scr, i2t_scr, wt_scr):
    nk = PEER_N_KEYS
    pair_pos = _pair_positions(ROUTE_TOK)

    def stage1(h):
        s1 = lax.dot_general(keys_ref[2 * h], q_ref[2 * h], _NT, preferred_element_type=F32)
        s2 = lax.dot_general(keys_ref[2 * h + 1], q_ref[2 * h + 1], _NT, preferred_element_type=F32)
        return _topk_columns(s1) + _topk_columns(s2)

    def stage2(h, tops):
        v1, p1, v2, p2 = tops
        cand_s = _pair_candidates(v1, v2, lambda a, b: a + b)
        cand_i = _pair_candidates(p1, p2, lambda a, b: a * nk + b)
        best, _, idx = _topk_rows(cand_s, pair_pos, cand_i)
        e = jnp.exp(best - best[0:1])
        gate = e / jnp.sum(e, axis=0, keepdims=True)
        r0 = pl.multiple_of(h * PEER_TOPK, PEER_TOPK)
        g_scr[pl.ds(r0, PEER_TOPK), :] = gate
        i1_scr[pl.ds(r0, PEER_TOPK), :] = lax.shift_right_logical(idx, 7).astype(F32)
        i2_scr[pl.ds(r0, PEER_TOPK), :] = lax.bitwise_and(idx, nk - 1).astype(F32)

    sub = lax.broadcasted_iota(jnp.int32, (nk, nk), 0).astype(F32)
    tok_per_head = ROUTE_TOK // PEER_HEADS

    def build_w(blk):
        for u in range(tok_per_head):
            t = blk * tok_per_head + u
            g_row = gt_scr[pl.ds(t, 1), :]
            lhs = jnp.where(i1t_scr[pl.ds(t, 1), :] == sub, g_row, 0.0).astype(BF16)
            rhs = jnp.where(i2t_scr[pl.ds(t, 1), :] == sub, 1.0, 0.0).astype(BF16)
            w_t = lax.dot_general(lhs, rhs, _NT, preferred_element_type=F32)
            wt_scr[pl.ds(pl.multiple_of(t * W_ROW_STRIDE, SUBLANES), nk), :] = w_t

    @pl.when(pl.program_id(0) == 0)
    def _():
        gt_scr[...] = jnp.zeros_like(gt_scr)
        i1t_scr[...] = jnp.zeros_like(i1t_scr)
        i2t_scr[...] = jnp.zeros_like(i2t_scr)

    def head(h, tops):
        nxt = stage1(h + 1)
        stage2(h, tops)
        build_w(h)
        return nxt

    stage2(PEER_HEADS - 1, lax.fori_loop(0, PEER_HEADS - 1, head, stage1(0)))
    build_w(PEER_HEADS - 1)

    def key_row(a, _):
        w_ref[a] = wt_scr[pl.ds(a, ROUTE_TOK, stride=W_ROW_STRIDE), :].astype(BF16)
        return 0

    lax.fori_loop(0, nk, key_row, 0, unroll=4)

    gt_scr[...] = g_scr[...].T
    i1t_scr[...] = i1_scr[...].T
    i2t_scr[...] = i2_scr[...].T


def _route(q3, keys):
    nhp, rows, sd = q3.shape
    nk = PEER_N_KEYS
    nblk = rows // ROUTE_TOK
    return pl.pallas_call(
        _route_kernel,
        grid=(nblk + 1,),
        in_specs=[
            pl.BlockSpec((nhp, ROUTE_TOK, sd), lambda i: (0, jnp.minimum(i, nblk - 1), 0)),
            pl.BlockSpec((nhp, nk, sd), lambda i: (0, 0, 0)),
        ],
        out_specs=pl.BlockSpec((nk, ROUTE_TOK, nk), lambda i: (0, jnp.maximum(i - 1, 0), 0)),
        out_shape=jax.ShapeDtypeStruct((nk, rows, nk), BF16),
        scratch_shapes=[pltpu.VMEM((PEER_HEADS * PEER_TOPK, ROUTE_TOK), F32)] * 3
        + [pltpu.VMEM((ROUTE_TOK, PEER_HEADS * PEER_TOPK), F32)] * 3
        + [pltpu.VMEM((ROUTE_TOK * W_ROW_STRIDE, nk), F32)],
        compiler_params=_params(("arbitrary",)),
        name="peer_route",
    )(q3, keys)


def _expert_kernel(hn_ref, u_ref, v_ref, w_ref, x1_ref, nw_ref, o_ref):
    j = pl.program_id(1)

    @pl.when(j == 0)
    def _():
        o_ref[...] = x1_ref[...]

    act = lax.dot_general(hn_ref[...], u_ref[...], _NT, preferred_element_type=F32)
    gel = 0.5 * act * (1.0 + lax.erf(act * (2.0 ** -0.5)))
    wts = jnp.concatenate([w_ref[a] for a in range(w_ref.shape[0])], axis=1)
    coef = (gel * wts.astype(F32)).astype(BF16)
    o_ref[...] += jnp.dot(coef, v_ref[...], preferred_element_type=F32)

    @pl.when(j == pl.num_programs(1) - 1)
    def _():
        y = o_ref[...]
        ms = jnp.mean(y * y, axis=-1, keepdims=True)
        o_ref[...] = y * lax.rsqrt(ms + EPS) * nw_ref[...]


def _experts(hn, u, v, w3, x1, norm_w, tb, te):
    rows, d = hn.shape
    ne = u.shape[0]
    ka = te // PEER_N_KEYS
    return pl.pallas_call(
        _expert_kernel,
        grid=(rows // tb, ne // te),
        in_specs=[
            pl.BlockSpec((tb, d), lambda i, j: (i, 0), pipeline_mode=pl.Buffered(1)),
            pl.BlockSpec((te, d), lambda i, j: (j, 0)),
            pl.BlockSpec((te, d), lambda i, j: (j, 0)),
            pl.BlockSpec((ka, tb, PEER_N_KEYS), lambda i, j: (j, i, 0)),
            pl.BlockSpec((tb, d), lambda i, j: (i, 0), pipeline_mode=pl.Buffered(1)),
            pl.BlockSpec((1, d), lambda i, j: (0, 0)),
        ],
        out_specs=pl.BlockSpec((tb, d), lambda i, j: (i, 0)),
        out_shape=jax.ShapeDtypeStruct((rows, d), F32),
        compiler_params=_params(("parallel", "arbitrary")),
        name="peer_experts",
    )(hn, u, v, w3, x1, norm_w)


def _rope_tables(p_len):
    pos = jnp.maximum(jnp.arange(p_len) - PAD_LEN, 0).astype(F32)
    inv_freq = ROPE_THETA ** (-jnp.arange(0, ATTN_HEAD_DIM, 2, dtype=F32) / ATTN_HEAD_DIM)
    ang = pos[:, None] * inv_freq[None, :]
    cos = jnp.tile(jnp.cos(ang), (1, 4))
    sin = jnp.sin(ang)
    return cos, jnp.tile(jnp.concatenate([-sin, sin], axis=-1), (1, 2))


def _pick(n, candidates):
    for c in candidates:
        if n % c == 0:
            return c
    raise ValueError(f"no tile in {candidates} divides {n}")


def kernel(x, meta_tokens, norm_mix_w, w_in, attn_lambda_qk, attn_subln_w, mlstm_conv_w, mlstm_conv_b, mlstm_i_b, mlstm_f_b, mlstm_norm_w, w_out, norm_ffn_w, peer_w_q, peer_sub_keys, peer_u, peer_v, norm_final_w):
    batch, seq, d = x.shape
    assert w_in.shape[0] == 1, "single-layer block"
    p_len = seq + Q_BLOCK
    lambda_init = 0.8 - 0.6 * math.exp(-0.3 * 0)
    n_main = w_in.shape[-1] - 2 * MLSTM_HEADS

    head_rows = jnp.concatenate([jnp.zeros((PAD_LEN, d), x.dtype), meta_tokens.astype(x.dtype)], axis=0)
    x2 = x.reshape(batch * seq, d)

    w_main = w_in[0, :, :n_main].astype(BF16)
    w_gate = jnp.pad(w_in[0, :, n_main:], ((0, 0), (0, LANES - 2 * MLSTM_HEADS))).astype(BF16)
    tn = _pick(n_main, (1024, 512))
    ztok, zg_tok = _inproj(x2, norm_mix_w, w_main, w_gate, tm=_pick(batch * seq, (1024, 512, 256, 128)), tn=tn)
    zmeta, zg_meta = _inproj(head_rows, norm_mix_w, w_main, w_gate, tm=Q_BLOCK, tn=tn)

    cos, sin = _rope_tables(p_len)
    tok_first = lambda t: jnp.concatenate([t[Q_BLOCK:], t[:Q_BLOCK]], axis=0)
    attn = _attention(ztok, zmeta, attn_lambda_qk[0], tok_first(cos), tok_first(sin), attn_subln_w, batch, seq,
                      lambda_init, tq=_pick(seq, (512, 256, 128)))

    nc = p_len // MLSTM_CHUNK
    rows = -(-nc // 8) * 8
    n_gate = 2 * MLSTM_HEADS
    gates = jnp.concatenate([jnp.broadcast_to(zg_meta[None, :, :n_gate], (batch, Q_BLOCK, n_gate)),
                             zg_tok[:, :n_gate].reshape(batch, seq, n_gate)], axis=1)
    gates = gates.reshape(batch, nc, MLSTM_CHUNK, 2, MLSTM_HEADS)
    gates = jnp.pad(gates.transpose(3, 0, 4, 1, 2), ((0, 0), (0, 0), (0, 0), (0, rows - nc), (0, 0)))
    gates = gates.reshape(2, batch * MLSTM_HEADS, rows, MLSTM_CHUNK)
    ml = _mlstm(ztok, zmeta, gates[0], gates[1], mlstm_i_b[0], mlstm_f_b[0], mlstm_conv_w[0], mlstm_conv_b,
                mlstm_norm_w, batch, seq)

    half = ATTN_HEADS * ATTN_V_DIM
    x1, hn, q3 = _outproj(attn.reshape(batch * seq, half), ml.reshape(batch * seq, d - half),
                          x2, w_out[0, :half].astype(BF16), w_out[0, half:].astype(BF16),
                          norm_ffn_w, peer_w_q[0].astype(BF16), tm=_pick(batch * seq, (512, 256, 128)))

    keys = peer_sub_keys[0].reshape(2 * PEER_HEADS, PEER_N_KEYS, PEER_SUB_DIM).astype(BF16)
    w3 = _route(q3, keys)

    out = _experts(hn, peer_u[0].astype(BF16), peer_v[0].astype(BF16), w3, x1, norm_final_w[None],
                   tb=_pick(batch * seq, (1024, 512, 256, 128)), te=1024)
    return out.reshape(batch, seq, d)
```

```python
import functools
import math

import jax
import jax.numpy as jnp
from jax import lax
from jax.experimental import pallas as pl
from jax.experimental.pallas import tpu as pltpu

F32 = jnp.float32
BF16 = jnp.bfloat16

N_META = 16
Q_BLOCK = 128
PAD_LEN = Q_BLOCK - N_META
ATTN_HEADS = 8
ATTN_HEAD_DIM = 64
ATTN_V_DIM = 128
ROPE_THETA = 10000.0
MLSTM_HEADS = 4
MLSTM_QK_DIM = 128
MLSTM_V_DIM = 256
MLSTM_CHUNK = 64
CONV_WIDTH = 4
PEER_HEADS = 8
PEER_N_KEYS = 128
PEER_SUB_DIM = 128
PEER_TOPK = 16
EPS = 1e-6
NEG = -1e30

LANES = 128
SUBLANES = 8
VMEM_LIMIT = 56 * 1024 * 1024

_NT = (((1,), (1,)), ((), ()))
_TN = (((0,), (0,)), ((), ()))


def _params(sem):
    return pltpu.CompilerParams(dimension_semantics=sem, vmem_limit_bytes=VMEM_LIMIT)


def _inproj_kernel(x_ref, nw_ref, w_ref, wg_ref, z_ref, zg_ref, h_scr):
    @pl.when(pl.program_id(1) == 0)
    def _():
        x = x_ref[...]
        ms = jnp.mean(x * x, axis=-1, keepdims=True)
        h_scr[...] = (x * lax.rsqrt(ms + EPS) * nw_ref[...]).astype(BF16)
        zg_ref[...] = jnp.dot(h_scr[...], wg_ref[...], preferred_element_type=F32)

    z_ref[...] = jnp.dot(h_scr[...], w_ref[...], preferred_element_type=F32).astype(BF16)


def _inproj(hp, norm_w, w_main, w_gate, tm, tn):
    rows, d = hp.shape
    cols = w_main.shape[1]
    return pl.pallas_call(
        _inproj_kernel,
        grid=(rows // tm, cols // tn),
        in_specs=[
            pl.BlockSpec((tm, d), lambda i, j: (i, 0)),
            pl.BlockSpec((1, d), lambda i, j: (0, 0)),
            pl.BlockSpec((d, tn), lambda i, j: (0, j)),
            pl.BlockSpec((d, LANES), lambda i, j: (0, 0)),
        ],
        out_specs=[
            pl.BlockSpec((tm, tn), lambda i, j: (i, j)),
            pl.BlockSpec((tm, LANES), lambda i, j: (i, 0)),
        ],
        out_shape=[
            jax.ShapeDtypeStruct((rows, cols), BF16),
            jax.ShapeDtypeStruct((rows, LANES), F32),
        ],
        scratch_shapes=[pltpu.VMEM((tm, d), BF16)],
        compiler_params=_params(("parallel", "arbitrary")),
        name="inproj",
    )(hp, norm_w, w_main, w_gate)


def _attn_kernel(lq_ref, q_ref, k_ref, v_ref, km_ref, vm_ref, cos_ref, sin_ref, sw_ref, o_ref,
                 q1t_scr, q2t_scr, k_scr, vtm_scr, vt_scr, *, seq, tq, cw, lambda_init):
    n_tiles = seq // tq

    def rope(x, r0):
        n = x.shape[0]
        low_half = (lax.broadcasted_iota(jnp.int32, x.shape, 1) % ATTN_HEAD_DIM) < (ATTN_HEAD_DIM // 2)
        rot = jnp.where(low_half, pltpu.roll(x, LANES - 32, 1), pltpu.roll(x, 32, 1))
        return x * cos_ref[r0:r0 + n, :] + rot * sin_ref[r0:r0 + n, :]

    first_map = lax.broadcasted_iota(jnp.int32, (seq, LANES), 1) < ATTN_HEAD_DIM
    q = rope(q_ref[0].astype(F32), 0) * (ATTN_HEAD_DIM ** -0.5)
    k_scr[0:seq, :] = rope(k_ref[0].astype(F32), 0).astype(BF16)
    k_scr[seq:seq + Q_BLOCK, :] = rope(km_ref[...].astype(F32), seq).astype(BF16)
    q1 = jnp.where(first_map, q, 0.0)
    q2 = jnp.where(first_map, 0.0, q)
    vtm_scr[...] = vm_ref[...].astype(F32).T.astype(BF16)
    for j in range(n_tiles):
        rows = slice(j * tq, (j + 1) * tq)
        q1t_scr[:, rows] = q1[rows].T.astype(BF16)
        q2t_scr[:, rows] = q2[rows].T.astype(BF16)
        vt_scr[j] = v_ref[0, rows, :].astype(F32).T.astype(BF16)

    lq = lq_ref[...]
    lam = (jnp.exp(jnp.sum(lq[0:1] * lq[1:2], axis=-1, keepdims=True))
           - jnp.exp(jnp.sum(lq[2:3] * lq[3:4], axis=-1, keepdims=True)) + lambda_init)

    n_chain = 2 * tq // cw

    def step(qts, kt, vtt, carry, masks):
        out = []
        for qt, (m, l, acc), mask in zip(qts, carry, masks):
            s = jnp.dot(kt, qt, preferred_element_type=F32)
            if mask is not None:
                s = jnp.where(mask, s, NEG)
            n = jnp.maximum(m, jnp.max(s, axis=0, keepdims=True))
            p = jnp.exp(s - n)
            c = jnp.exp(m - n)
            l = c * l + jnp.sum(p, axis=0, keepdims=True)
            acc = c * acc + jnp.dot(vtt, p.astype(BF16), preferred_element_type=F32)
            out.append((n, l, acc))
        return tuple(out)

    no_masks = (None,) * n_chain
    meta_masks = (lax.broadcasted_iota(jnp.int32, (Q_BLOCK, cw), 0) >= PAD_LEN,) * n_chain
    key_io = lax.broadcasted_iota(jnp.int32, (tq, cw), 0)
    qry_io = lax.broadcasted_iota(jnp.int32, (tq, cw), 1)
    diag_masks = tuple(key_io <= lax.bitwise_and(qry_io + c * cw, tq - 1) for c in range(n_chain))

    for i in range(n_tiles):
        cols = slice(i * tq, (i + 1) * tq)
        q12t = jnp.concatenate([q1t_scr[:, cols], q2t_scr[:, cols]], axis=1)
        qts = tuple(q12t[:, c * cw:(c + 1) * cw] for c in range(n_chain))
        carry = tuple((jnp.full((1, cw), NEG, F32), jnp.zeros((1, cw), F32), jnp.zeros((ATTN_V_DIM, cw), F32))
                      for _ in range(n_chain))
        carry = step(qts, k_scr[seq:seq + Q_BLOCK, :], vtm_scr[...], carry, meta_masks)

        def full(j, c, qts=qts):
            k0 = pl.multiple_of(j * tq, tq)
            return step(qts, k_scr[pl.ds(k0, tq), :], vt_scr[j], c, no_masks)

        carry = lax.fori_loop(0, i, full, carry)
        carry = step(qts, k_scr[cols, :], vt_scr[i], carry, diag_masks)
        a = jnp.concatenate([acc / l for _, l, acc in carry], axis=1)
        o = (a[:, :tq] - lam * a[:, tq:]).T
        ms = jnp.mean(o * o, axis=-1, keepdims=True)
        o = o * lax.rsqrt(ms + EPS) * sw_ref[...] * (1.0 - lambda_init)
        o_ref[0, cols, :] = o.astype(BF16)


def _attention(ztok, zmeta, lam_qk, cos, sin, subln_w, batch, seq, lambda_init, tq):
    assert tq & (tq - 1) == 0, "query tile must be a power of two"
    p_len = seq + Q_BLOCK
    z3 = ztok.reshape(batch, seq, ztok.shape[-1])
    kern = functools.partial(_attn_kernel, seq=seq, tq=tq, cw=2 * tq, lambda_init=lambda_init)
    blk = lambda off: pl.BlockSpec((1, seq, LANES), lambda b, h, off=off: (b, 0, off + h))
    mblk = lambda off: pl.BlockSpec((Q_BLOCK, LANES), lambda b, h, off=off: (0, off + h))
    const = lambda shape: pl.BlockSpec(shape, lambda b, h: (0,) * len(shape))
    return pl.pallas_call(
        kern,
        grid=(batch, ATTN_HEADS),
        in_specs=[
            const((4, ATTN_HEAD_DIM)),
            blk(0), blk(ATTN_HEADS), blk(2 * ATTN_HEADS),
            mblk(ATTN_HEADS), mblk(2 * ATTN_HEADS),
            const((p_len, LANES)), const((p_len, LANES)), const((1, ATTN_V_DIM)),
        ],
        out_specs=pl.BlockSpec((1, seq, ATTN_V_DIM), lambda b, h: (b, 0, h)),
        out_shape=jax.ShapeDtypeStruct((batch, seq, ATTN_HEADS * ATTN_V_DIM), BF16),
        scratch_shapes=[
            pltpu.VMEM((LANES, seq), BF16),
            pltpu.VMEM((LANES, seq), BF16),
            pltpu.VMEM((p_len, LANES), BF16),
            pltpu.VMEM((ATTN_V_DIM, Q_BLOCK), BF16),
            pltpu.VMEM((seq // tq, ATTN_V_DIM, tq), BF16),
        ],
        compiler_params=_params(("parallel", "parallel")),
        name="diff_attn",
    )(lam_qk, z3, z3, z3, zmeta, zmeta, cos, sin, subln_w)


def _mlstm_kernel(ib_ref, fb_ref, q_ref, k_ref, v_ref, og_ref, qm_ref, km_ref, vm_ref,
                  cwq_ref, cwk_ref, cbq_ref, cbk_ref, gi_ref, gf_ref, nw_ref, o_ref,
                  pad_scr, q_scr, k_scr, b_scr, li_scr, c_scr, *, seq):
    p_len = seq + Q_BLOCK
    nc = p_len // MLSTM_CHUNK
    cl = MLSTM_CHUNK
    nh = MLSTM_HEADS
    dk, dv = MLSTM_QK_DIM, MLSTM_V_DIM

    def conv_silu(x_ref, xm_ref, w_ref, b_ref, h):
        cols = slice(h * dk, (h + 1) * dk)
        pad_scr[0:8, :] = jnp.zeros((8, dk), F32)
        pad_scr[8:8 + Q_BLOCK, :] = xm_ref[:, cols].astype(F32)
        pad_scr[8 + Q_BLOCK:8 + p_len, :] = x_ref[0, :, cols].astype(F32)
        y = b_ref[:, cols] + w_ref[3:4, cols] * pad_scr[8:8 + p_len, :]
        for j in range(CONV_WIDTH - 1):
            y = y + w_ref[j:j + 1, cols] * pad_scr[5 + j:5 + j + p_len, :]
        return y * (1.0 / (1.0 + jnp.exp(-y)))

    rows = gi_ref.shape[1]
    pos = (lax.broadcasted_iota(jnp.int32, (rows, cl), 0) * cl
           + lax.broadcasted_iota(jnp.int32, (rows, cl), 1))
    valid = pos >= PAD_LEN
    upper = (lax.broadcasted_iota(jnp.int32, (cl, cl), 0)
             <= lax.broadcasted_iota(jnp.int32, (cl, cl), 1)).astype(F32)
    for h in range(nh):
        q_scr[:, h * dk:(h + 1) * dk] = (conv_silu(q_ref, qm_ref, cwq_ref, cbq_ref, h) * (dk ** -0.5)).astype(BF16)
        k_scr[:, h * dk:(h + 1) * dk] = conv_silu(k_ref, km_ref, cwk_ref, cbk_ref, h)
        fpre = gf_ref[h] + fb_ref[h]
        log_f = jnp.minimum(fpre, 0.0) - jnp.log(1.0 + jnp.exp(-jnp.abs(fpre)))
        log_f = jnp.where(valid, log_f, 0.0)
        li_scr[h] = jnp.where(valid, gi_ref[h] + ib_ref[h], NEG)
        b_scr[h] = jnp.dot(log_f, upper, preferred_element_type=F32,
                           precision=lax.Precision.HIGHEST)

    c_scr[...] = jnp.zeros_like(c_scr)
    r_io = lax.broadcasted_iota(jnp.int32, (cl, cl), 0)
    c_io = lax.broadcasted_iota(jnp.int32, (cl, cl), 1)
    eye = r_io == c_io
    tril = c_io <= r_io

    def to_col(row):
        return jnp.sum(jnp.where(eye, row, 0.0), axis=1, keepdims=True)

    def head_chunk(h, c, r0, m, n, in_meta):
        b_r = b_scr[h, pl.ds(c, 1), :]
        li_r = li_scr[h, pl.ds(c, 1), :]
        g = b_r[:, cl - 1:cl]
        a_r = g - b_r + li_r
        b_c = to_col(b_r)
        a_c = to_col(a_r)
        dmat = jnp.where(tril, b_c - b_r + li_r, NEG)
        m_inter = b_c + m
        m_t = jnp.maximum(jnp.max(dmat, axis=1, keepdims=True), m_inter)
        qc = q_scr[pl.ds(r0, cl), h * dk:(h + 1) * dk]
        kc = k_scr[pl.ds(r0, cl), h * dk:(h + 1) * dk]
        if in_meta:
            vc = vm_ref[r0:r0 + cl, h * dv:(h + 1) * dv]
        else:
            t0 = pl.multiple_of(r0 - Q_BLOCK, cl)
            vc = v_ref[0, pl.ds(t0, cl), h * dv:(h + 1) * dv]
        s = lax.dot_general(qc, kc.astype(BF16), _NT, preferred_element_type=F32) * jnp.exp(dmat - m_t)
        inter_w = jnp.exp(m_inter - m_t)
        num = (jnp.dot(s.astype(BF16), vc, preferred_element_type=F32)
               + inter_w * jnp.dot(qc, c_scr[h].astype(BF16), preferred_element_type=F32))
        nq = (jnp.sum(s, axis=1, keepdims=True)
              + inter_w * jnp.sum(qc.astype(F32) * n, axis=1, keepdims=True))
        if not in_meta:
            hid = num / jnp.maximum(jnp.abs(nq), jnp.exp(-m_t))
            ms = jnp.mean(hid * hid, axis=-1, keepdims=True)
            hn = hid * lax.rsqrt(ms + EPS) * nw_ref[:, h * dv:(h + 1) * dv]
            og = og_ref[0, pl.ds(t0, cl), h * dv:(h + 1) * dv].astype(F32)
            out = hn * (1.0 / (1.0 + jnp.exp(-og)))
            o_ref[0, pl.ds(t0, cl), h * dv:(h + 1) * dv] = out.astype(BF16)

        m_new = jnp.maximum(g + m, jnp.max(a_r, axis=1, keepdims=True))
        w_c = jnp.exp(a_c - m_new)
        decay = jnp.exp(g + m - m_new)
        kw = kc * w_c
        c_scr[h] = decay * c_scr[h] + lax.dot_general(kw.astype(BF16), vc, _TN, preferred_element_type=F32)
        n_new = decay * n + jnp.sum(kw, axis=0, keepdims=True)
        return m_new, n_new

    def chunk(c, carry):
        r0 = pl.multiple_of(c * cl, cl)
        return tuple(head_chunk(h, c, r0, *carry[h], in_meta=False) for h in range(nh))

    carry = tuple((jnp.full((1, 1), NEG, F32), jnp.zeros((1, dk), F32)) for _ in range(nh))
    n_meta = Q_BLOCK // cl
    for c in range(n_meta):
        carry = tuple(head_chunk(h, c, c * cl, *carry[h], in_meta=True) for h in range(nh))
    lax.fori_loop(n_meta, nc, chunk, carry, unroll=2 if (nc - n_meta) % 2 == 0 else 1)


def _mlstm(ztok, zmeta, gi_rows, gf_rows, i_b, f_b, conv_w, conv_b, norm_w, batch, seq):
    p_len = seq + Q_BLOCK
    z3 = ztok.reshape(batch, seq, ztok.shape[-1])
    nh = MLSTM_HEADS
    rows = gi_rows.shape[1]
    qk_w = nh * MLSTM_QK_DIM
    v_w = nh * MLSTM_V_DIM
    q_off = 3 * ATTN_HEADS * LANES // qk_w
    v_off = (q_off + 2) * qk_w // v_w
    smem = pl.BlockSpec(memory_space=pltpu.SMEM)
    return pl.pallas_call(
        functools.partial(_mlstm_kernel, seq=seq),
        grid=(batch,),
        in_specs=[
            smem, smem,
            pl.BlockSpec((1, seq, qk_w), lambda b: (b, 0, q_off)),
            pl.BlockSpec((1, seq, qk_w), lambda b: (b, 0, q_off + 1)),
            pl.BlockSpec((1, seq, v_w), lambda b: (b, 0, v_off)),
            pl.BlockSpec((1, seq, v_w), lambda b: (b, 0, v_off + 1)),
            pl.BlockSpec((Q_BLOCK, qk_w), lambda b: (0, q_off)),
            pl.BlockSpec((Q_BLOCK, qk_w), lambda b: (0, q_off + 1)),
            pl.BlockSpec((Q_BLOCK, v_w), lambda b: (0, v_off)),
            pl.BlockSpec((CONV_WIDTH, qk_w), lambda b: (0, 0)),
            pl.BlockSpec((CONV_WIDTH, qk_w), lambda b: (0, 1)),
            pl.BlockSpec((1, qk_w), lambda b: (0, 0)),
            pl.BlockSpec((1, qk_w), lambda b: (0, 1)),
            pl.BlockSpec((nh, rows, MLSTM_CHUNK), lambda b: (b, 0, 0)),
            pl.BlockSpec((nh, rows, MLSTM_CHUNK), lambda b: (b, 0, 0)),
            pl.BlockSpec((1, v_w), lambda b: (0, 0)),
        ],
        out_specs=pl.BlockSpec((1, seq, v_w), lambda b: (b, 0, 0)),
        out_shape=jax.ShapeDtypeStruct((batch, seq, v_w), BF16),
        scratch_shapes=[
            pltpu.VMEM((p_len + 8, MLSTM_QK_DIM), F32),
            pltpu.VMEM((p_len, qk_w), BF16),
            pltpu.VMEM((p_len, qk_w), F32),
            pltpu.VMEM((nh, rows, MLSTM_CHUNK), F32),
            pltpu.VMEM((nh, rows, MLSTM_CHUNK), F32),
            pltpu.VMEM((nh, MLSTM_QK_DIM, MLSTM_V_DIM), F32),
        ],
        compiler_params=_params(("parallel",)),
        name="mlstm",
    )(i_b, f_b, z3, z3, z3, z3, zmeta, zmeta, zmeta, conv_w, conv_w, conv_b, conv_b, gi_rows, gf_rows, norm_w)


def _outproj_kernel(a_ref, m_ref, x_ref, wa_ref, wm_ref, nw_ref, wq_ref, x1_ref, hn_ref, q_ref):
    x1 = (x_ref[...]
          + jnp.dot(a_ref[...], wa_ref[...], preferred_element_type=F32)
          + jnp.dot(m_ref[...], wm_ref[...], preferred_element_type=F32))
    x1_ref[...] = x1
    ms = jnp.mean(x1 * x1, axis=-1, keepdims=True)
    hn = (x1 * lax.rsqrt(ms + EPS) * nw_ref[...]).astype(BF16)
    hn_ref[...] = hn
    q = jnp.dot(hn, wq_ref[...], preferred_element_type=F32).astype(BF16)
    for hp in range(2 * PEER_HEADS):
        q_ref[hp] = q[:, hp * PEER_SUB_DIM:(hp + 1) * PEER_SUB_DIM]


def _outproj(attn, ml, x2, w_attn, w_ml, norm_w, w_q, tm):
    rows, d = x2.shape
    half = attn.shape[1]
    single = pl.Buffered(1)
    return pl.pallas_call(
        _outproj_kernel,
        grid=(rows // tm,),
        in_specs=[
            pl.BlockSpec((tm, half), lambda i: (i, 0)),
            pl.BlockSpec((tm, half), lambda i: (i, 0)),
            pl.BlockSpec((tm, d), lambda i: (i, 0)),
            pl.BlockSpec((half, d), lambda i: (0, 0), pipeline_mode=single),
            pl.BlockSpec((half, d), lambda i: (0, 0), pipeline_mode=single),
            pl.BlockSpec((1, d), lambda i: (0, 0)),
            pl.BlockSpec((d, d), lambda i: (0, 0), pipeline_mode=single),
        ],
        out_specs=[
            pl.BlockSpec((tm, d), lambda i: (i, 0)),
            pl.BlockSpec((tm, d), lambda i: (i, 0)),
            pl.BlockSpec((2 * PEER_HEADS, tm, PEER_SUB_DIM), lambda i: (0, i, 0)),
        ],
        out_shape=[
            jax.ShapeDtypeStruct((rows, d), F32),
            jax.ShapeDtypeStruct((rows, d), BF16),
            jax.ShapeDtypeStruct((2 * PEER_HEADS, rows, PEER_SUB_DIM), BF16),
        ],
        compiler_params=_params(("parallel",)),
        name="outproj",
    )(attn, ml, x2, w_attn, w_ml, norm_w, w_q)


ROUTE_TOK = LANES
W_ROW_STRIDE = PEER_N_KEYS + 8


def _sorting_network(n):
    pairs, p = [], 1
    while p < n:
        k = p
        while k >= 1:
            for j in range(k % p, n - k, 2 * k):
                for i in range(min(k, n - j - k)):
                    if (i + j) // (2 * p) == (i + j + k) // (2 * p):
                        pairs.append((i + j, i + j + k))
            k //= 2
        p *= 2
    return pairs


def _topk_columns(s):
    n_lvl = s.shape[0] // SUBLANES
    vs = [s[l * SUBLANES:(l + 1) * SUBLANES] for l in range(n_lvl)]
    col = lax.broadcasted_iota(jnp.int32, vs[0].shape, 0)
    ix = [col + l * SUBLANES for l in range(n_lvl)]
    for a, b in _sorting_network(n_lvl):
        va, vb, ia, ib = vs[a], vs[b], ix[a], ix[b]
        gt, lt = va > vb, va < vb
        vs[a], vs[b] = jnp.maximum(va, vb), jnp.minimum(va, vb)
        ix[a] = jnp.where(gt, ia, jnp.where(lt, ib, jnp.minimum(ia, ib)))
        ix[b] = jnp.where(gt, ib, jnp.where(lt, ia, jnp.maximum(ia, ib)))
    sentinel = jnp.int32(2 ** 30)
    k_io = lax.broadcasted_iota(jnp.int32, (PEER_TOPK, s.shape[1]), 0)
    vals = jnp.zeros((PEER_TOPK, s.shape[1]), F32)
    rows = jnp.zeros((PEER_TOPK, s.shape[1]), jnp.int32)
    for it in range(PEER_TOPK):
        m = jnp.max(vs[0], axis=0, keepdims=True)
        r = jnp.min(jnp.where(vs[0] == m, ix[0], sentinel), axis=0, keepdims=True)
        vals = jnp.where(k_io == it, m, vals)
        rows = jnp.where(k_io == it, r, rows)
        hit = ix[0] == r
        for l in range(PEER_TOPK - 1 - it):
            vs[l] = jnp.where(hit, vs[l + 1], vs[l])
            ix[l] = jnp.where(hit, ix[l + 1], ix[l])
    return vals, rows


def _pair_topk(v1, p1, v2, p2):
    k, sl, nk = PEER_TOPK, SUBLANES, PEER_N_KEYS
    col = lax.broadcasted_iota(jnp.int32, (sl, v1.shape[1]), 0)
    main_v = [v1[i:i + 1] + v2[0:sl] for i in range(sl)]
    main_p = [p1[i:i + 1] * nk + p2[0:sl] for i in range(sl)]
    level = jnp.zeros_like(col)
    row_v, row_p, row_pos = v1[0:1] + v2[sl:], p1[0:1] * nk + p2[sl:], col + sl
    tail_v, tail_p, tail_pos = v1[sl:] + v2[0:1], p1[sl:] * nk + p2[0:1], (col + sl) * k
    sentinel = jnp.int32(2 ** 30)
    k_io = lax.broadcasted_iota(jnp.int32, (k, v1.shape[1]), 0)
    best = jnp.zeros((k, v1.shape[1]), F32)
    idx = jnp.zeros((k, v1.shape[1]), jnp.int32)
    for it in range(k):
        main_pos = level * k + col
        m = jnp.max(jnp.maximum(jnp.maximum(main_v[0], row_v), tail_v), axis=0, keepdims=True)
        cand_m = jnp.where(main_v[0] == m, main_pos, sentinel)
        cand_r = jnp.where(row_v == m, row_pos, sentinel)
        cand_t = jnp.where(tail_v == m, tail_pos, sentinel)
        pos = jnp.min(jnp.minimum(jnp.minimum(cand_m, cand_r), cand_t), axis=0, keepdims=True)
        hit_m, hit_r, hit_t = cand_m == pos, cand_r == pos, cand_t == pos
        pay = jnp.max(jnp.maximum(jnp.maximum(jnp.where(hit_m, main_p[0], -1), jnp.where(hit_r, row_p, -1)),
                                  jnp.where(hit_t, tail_p, -1)), axis=0, keepdims=True)
        best = jnp.where(k_io == it, m, best)
        idx = jnp.where(k_io == it, pay, idx)
        left = k - 1 - it
        for l in range(min(sl - 1, left)):
            main_v[l] = jnp.where(hit_m, main_v[l + 1], main_v[l])
            main_p[l] = jnp.where(hit_m, main_p[l + 1], main_p[l])
        if left >= sl - 1:
            main_v[sl - 1] = jnp.where(hit_m, -jnp.inf, main_v[sl - 1])
        level = jnp.where(hit_m, level + 1, level)
        row_v = jnp.where(hit_r, -jnp.inf, row_v)
        tail_v = jnp.where(hit_t, -jnp.inf, tail_v)
    return best, idx


def _route_kernel(q_ref, keys_ref, w_ref, g_scr, i1_scr, i2_scr, gt_scr, i1t_scr, i2t_scr, wt_scr):
    nk = PEER_N_KEYS

    def stage1(h):
        s1 = lax.dot_general(keys_ref[2 * h], q_ref[2 * h], _NT, preferred_element_type=F32)
        s2 = lax.dot_general(keys_ref[2 * h + 1], q_ref[2 * h + 1], _NT, preferred_element_type=F32)
        return _topk_columns(s1) + _topk_columns(s2)

    def stage2(h, tops):
        best, idx = _pair_topk(*tops)
        e = jnp.exp(best - best[0:1])
        gate = e / jnp.sum(e, axis=0, keepdims=True)
        r0 = pl.multiple_of(h * PEER_TOPK, PEER_TOPK)
        g_scr[pl.ds(r0, PEER_TOPK), :] = gate
        i1_scr[pl.ds(r0, PEER_TOPK), :] = lax.shift_right_logical(idx, 7).astype(F32)
        i2_scr[pl.ds(r0, PEER_TOPK), :] = lax.bitwise_and(idx, nk - 1).astype(F32)

    sub = lax.broadcasted_iota(jnp.int32, (nk, nk), 0).astype(F32)
    tok_per_head = ROUTE_TOK // PEER_HEADS

    def build_w(blk):
        for u in range(tok_per_head):
            t = blk * tok_per_head + u
            g_row = gt_scr[pl.ds(t, 1), :]
            lhs = jnp.where(i1t_scr[pl.ds(t, 1), :] == sub, g_row, 0.0).astype(BF16)
            rhs = jnp.where(i2t_scr[pl.ds(t, 1), :] == sub, 1.0, 0.0).astype(BF16)
            w_t = lax.dot_general(lhs, rhs, _NT, preferred_element_type=F32)
            wt_scr[pl.ds(pl.multiple_of(t * W_ROW_STRIDE, SUBLANES), nk), :] = w_t

    @pl.when(pl.program_id(0) == 0)
    def _():
        gt_scr[...] = jnp.zeros_like(gt_scr)
        i1t_scr[...] = jnp.zeros_like(i1t_scr)
        i2t_scr[...] = jnp.zeros_like(i2t_scr)

    def head(h, tops):
        nxt = stage1(h + 1)
        stage2(h, tops)
        build_w(h)
        return nxt

    stage2(PEER_HEADS - 1, lax.fori_loop(0, PEER_HEADS - 1, head, stage1(0)))
    build_w(PEER_HEADS - 1)

    def key_row(a, _):
        w_ref[a] = wt_scr[pl.ds(a, ROUTE_TOK, stride=W_ROW_STRIDE), :].astype(BF16)
        return 0

    lax.fori_loop(0, nk, key_row, 0, unroll=4)

    gt_scr[...] = g_scr[...].T
    i1t_scr[...] = i1_scr[...].T
    i2t_scr[...] = i2_scr[...].T


def _route(q3, keys):
    nhp, rows, sd = q3.shape
    nk = PEER_N_KEYS
    nblk = rows // ROUTE_TOK
    return pl.pallas_call(
        _route_kernel,
        grid=(nblk + 1,),
        in_specs=[
            pl.BlockSpec((nhp, ROUTE_TOK, sd), lambda i: (0, jnp.minimum(i, nblk - 1), 0)),
            pl.BlockSpec((nhp, nk, sd), lambda i: (0, 0, 0)),
        ],
        out_specs=pl.BlockSpec((nk, ROUTE_TOK, nk), lambda i: (0, jnp.maximum(i - 1, 0), 0)),
        out_shape=jax.ShapeDtypeStruct((nk, rows, nk), BF16),
        scratch_shapes=[pltpu.VMEM((PEER_HEADS * PEER_TOPK, ROUTE_TOK), F32)] * 3
        + [pltpu.VMEM((ROUTE_TOK, PEER_HEADS * PEER_TOPK), F32)] * 3
        + [pltpu.VMEM((ROUTE_TOK * W_ROW_STRIDE, nk), F32)],
        compiler_params=_params(("arbitrary",)),
        name="peer_route",
    )(q3, keys)


def _expert_kernel(hn_ref, u_ref, v_ref, w_ref, x1_ref, nw_ref, o_ref):
    j = pl.program_id(1)

    @pl.when(j == 0)
    def _():
        o_ref[...] = x1_ref[...]

    act = lax.dot_general(hn_ref[...], u_ref[...], _NT, preferred_element_type=F32)
    gel = 0.5 * act * (1.0 + lax.erf(act * (2.0 ** -0.5)))
    wts = jnp.concatenate([w_ref[a] for a in range(w_ref.shape[0])], axis=1)
    coef = (gel * wts.astype(F32)).astype(BF16)
    o_ref[...] += jnp.dot(coef, v_ref[...], preferred_element_type=F32)

    @pl.when(j == pl.num_programs(1) - 1)
    def _():
        y = o_ref[...]
        ms = jnp.mean(y * y, axis=-1, keepdims=True)
        o_ref[...] = y * lax.rsqrt(ms + EPS) * nw_ref[...]


def _experts(hn, u, v, w3, x1, norm_w, tb, te):
    rows, d = hn.shape
    ne = u.shape[0]
    ka = te // PEER_N_KEYS
    return pl.pallas_call(
        _expert_kernel,
        grid=(rows // tb, ne // te),
        in_specs=[
            pl.BlockSpec((tb, d), lambda i, j: (i, 0), pipeline_mode=pl.Buffered(1)),
            pl.BlockSpec((te, d), lambda i, j: (j, 0)),
            pl.BlockSpec((te, d), lambda i, j: (j, 0)),
            pl.BlockSpec((ka, tb, PEER_N_KEYS), lambda i, j: (j, i, 0)),
            pl.BlockSpec((tb, d), lambda i, j: (i, 0), pipeline_mode=pl.Buffered(1)),
            pl.BlockSpec((1, d), lambda i, j: (0, 0)),
        ],
        out_specs=pl.BlockSpec((tb, d), lambda i, j: (i, 0)),
        out_shape=jax.ShapeDtypeStruct((rows, d), F32),
        compiler_params=_params(("parallel", "arbitrary")),
        name="peer_experts",
    )(hn, u, v, w3, x1, norm_w)


def _rope_tables(p_len):
    pos = jnp.maximum(jnp.arange(p_len) - PAD_LEN, 0).astype(F32)
    inv_freq = ROPE_THETA ** (-jnp.arange(0, ATTN_HEAD_DIM, 2, dtype=F32) / ATTN_HEAD_DIM)
    ang = pos[:, None] * inv_freq[None, :]
    cos = jnp.tile(jnp.cos(ang), (1, 4))
    sin = jnp.sin(ang)
    return cos, jnp.tile(jnp.concatenate([-sin, sin], axis=-1), (1, 2))


def _pick(n, candidates):
    for c in candidates:
        if n % c == 0:
            return c
    raise ValueError(f"no tile in {candidates} divides {n}")


def kernel(x, meta_tokens, norm_mix_w, w_in, attn_lambda_qk, attn_subln_w, mlstm_conv_w, mlstm_conv_b, mlstm_i_b, mlstm_f_b, mlstm_norm_w, w_out, norm_ffn_w, peer_w_q, peer_sub_keys, peer_u, peer_v, norm_final_w):
    batch, seq, d = x.shape
    assert w_in.shape[0] == 1, "single-layer block"
    p_len = seq + Q_BLOCK
    lambda_init = 0.8 - 0.6 * math.exp(-0.3 * 0)
    n_main = w_in.shape[-1] - 2 * MLSTM_HEADS

    head_rows = jnp.concatenate([jnp.zeros((PAD_LEN, d), x.dtype), meta_tokens.astype(x.dtype)], axis=0)
    x2 = x.reshape(batch * seq, d)

    w_main = w_in[0, :, :n_main].astype(BF16)
    w_gate = jnp.pad(w_in[0, :, n_main:], ((0, 0), (0, LANES - 2 * MLSTM_HEADS))).astype(BF16)
    tn = _pick(n_main, (1024, 512))
    ztok, zg_tok = _inproj(x2, norm_mix_w, w_main, w_gate, tm=_pick(batch * seq, (1024, 512, 256, 128)), tn=tn)
    zmeta, zg_meta = _inproj(head_rows, norm_mix_w, w_main, w_gate, tm=Q_BLOCK, tn=tn)

    cos, sin = _rope_tables(p_len)
    tok_first = lambda t: jnp.concatenate([t[Q_BLOCK:], t[:Q_BLOCK]], axis=0)
    attn = _attention(ztok, zmeta, attn_lambda_qk[0], tok_first(cos), tok_first(sin), attn_subln_w, batch, seq,
                      lambda_init, tq=_pick(seq, (512, 256, 128)))

    nc = p_len // MLSTM_CHUNK
    rows = -(-nc // 8) * 8
    n_gate = 2 * MLSTM_HEADS
    gates = jnp.concatenate([jnp.broadcast_to(zg_meta[None, :, :n_gate], (batch, Q_BLOCK, n_gate)),
                             zg_tok[:, :n_gate].reshape(batch, seq, n_gate)], axis=1)
    gates = gates.reshape(batch, nc, MLSTM_CHUNK, 2, MLSTM_HEADS)
    gates = jnp.pad(gates.transpose(3, 0, 4, 1, 2), ((0, 0), (0, 0), (0, 0), (0, rows - nc), (0, 0)))
    gates = gates.reshape(2, batch * MLSTM_HEADS, rows, MLSTM_CHUNK)
    ml = _mlstm(ztok, zmeta, gates[0], gates[1], mlstm_i_b[0], mlstm_f_b[0], mlstm_conv_w[0], mlstm_conv_b,
                mlstm_norm_w, batch, seq)

    half = ATTN_HEADS * ATTN_V_DIM
    x1, hn, q3 = _outproj(attn.reshape(batch * seq, half), ml.reshape(batch * seq, d - half),
                          x2, w_out[0, :half].astype(BF16), w_out[0, half:].astype(BF16),
                          norm_ffn_w, peer_w_q[0].astype(BF16), tm=_pick(batch * seq, (512, 256, 128)))

    keys = peer_sub_keys[0].reshape(2 * PEER_HEADS, PEER_N_KEYS, PEER_SUB_DIM).astype(BF16)
    w3 = _route(q3, keys)

    out = _experts(hn, peer_u[0].astype(BF16), peer_v[0].astype(BF16), w3, x1, norm_final_w[None],
                   tb=_pick(batch * seq, (1024, 512, 256, 128)), te=1024)
    return out.reshape(batch, seq, d)
```
